```python
import jax, jax.numpy as jnp
from jax import lax
import numpy as np

D_MODEL = 4096
BATCH = 4
SEQ = 2048
DEPTH = 2

MIX_WIDTH = D_MODEL
HEAD_DIM = 128
MOBA_HEADS = (3 * MIX_WIDTH // 8) // HEAD_DIM
MOBA_BLOCK = 256
MOBA_TOPK = 3
MOBA_QCHUNK = 16
GLA_DV = 128
GLA_HEADS = (MIX_WIDTH // 4) // GLA_DV
GLA_DK = GLA_DV // 2
GLA_LOWRANK = 16
GLA_TAU = 16.0
GLA_CHUNK = 64
SWA_HEADS = (MIX_WIDTH - MOBA_HEADS * HEAD_DIM - GLA_HEADS * GLA_DV) // HEAD_DIM
SWA_KV_HEADS = 4
SWA_WINDOW = 128
D_FF = 4 * D_MODEL
ALIBI_MAX_EXP = 8.0
EPS = 1e-6

SPLIT_SIZES = (
    MOBA_HEADS * HEAD_DIM, MOBA_HEADS * HEAD_DIM, MOBA_HEADS * HEAD_DIM,
    GLA_HEADS * GLA_DK, GLA_HEADS * GLA_DK, GLA_HEADS * GLA_DV,
    GLA_HEADS * GLA_DV, GLA_LOWRANK,
    SWA_HEADS * HEAD_DIM, SWA_KV_HEADS * HEAD_DIM, SWA_KV_HEADS * HEAD_DIM,
)
D_IN = sum(SPLIT_SIZES)

kernel_name = "hybrid_moba_gla_swa_adaln_block"


def rms_norm(x, gain):
    xf = x.astype(jnp.float32)
    y = xf * lax.rsqrt(jnp.mean(xf * xf, axis=-1, keepdims=True) + EPS)
    return (y * gain.astype(jnp.float32)).astype(x.dtype)


def alibi_slopes(n_heads):
    return jnp.exp2(-ALIBI_MAX_EXP * jnp.arange(1, n_heads + 1, dtype=jnp.float32) / n_heads)


def moba_attention(q, k, v, slopes):
    B, H, S, D = q.shape
    nb = -(-S // MOBA_BLOCK)
    pad = nb * MOBA_BLOCK - S
    kb = jnp.pad(k, ((0, 0), (0, 0), (0, pad), (0, 0))).reshape(B, H, nb, MOBA_BLOCK, D)
    vb = jnp.pad(v, ((0, 0), (0, 0), (0, pad), (0, 0))).reshape(B, H, nb, MOBA_BLOCK, D)
    kmean = jnp.mean(kb.astype(jnp.float32), axis=3)
    pos = jnp.arange(S)
    qblk = pos // MOBA_BLOCK
    gate = jnp.einsum('bhsd,bhnd->bhsn', q.astype(jnp.float32), kmean)
    past = jnp.arange(nb)[None, :] < qblk[:, None]
    gate = jnp.where(past[None, None], gate, -jnp.inf)
    n_sel = min(MOBA_TOPK, nb)
    _, sel = lax.top_k(gate, n_sel)
    sel_valid = sel < qblk[None, None, :, None]
    scale = D ** -0.5
    bi = jnp.arange(B)[:, None, None, None]
    hi = jnp.arange(H)[None, :, None, None]
    offs = jnp.arange(MOBA_BLOCK)
    n_chunks = S // MOBA_QCHUNK

    def chunk(ci):
        t0 = ci * MOBA_QCHUNK
        qc = lax.dynamic_slice_in_dim(q, t0, MOBA_QCHUNK, axis=2)
        selc = lax.dynamic_slice_in_dim(sel, t0, MOBA_QCHUNK, axis=2)
        validc = lax.dynamic_slice_in_dim(sel_valid, t0, MOBA_QCHUNK, axis=2)
        tq = t0 + jnp.arange(MOBA_QCHUNK)
        own = t0 // MOBA_BLOCK
        k_sel = kb[bi, hi, selc]
        v_sel = vb[bi, hi, selc]
        k_own = lax.dynamic_index_in_dim(kb, own, axis=2, keepdims=False)
        v_own = lax.dynamic_index_in_dim(vb, own, axis=2, keepdims=False)
        s_sel = selc[..., None] * MOBA_BLOCK + offs
        s_own = own * MOBA_BLOCK + offs
        d_sel = (tq[None, None, :, None, None] - s_sel).astype(jnp.float32)
        l_sel = (jnp.einsum('bhqd,bhqnkd->bhqnk', qc, k_sel).astype(jnp.float32) * scale
                 - slopes[None, :, None, None, None] * d_sel)
        l_sel = jnp.where(validc[..., None], l_sel, -jnp.inf)
        d_own = (tq[:, None] - s_own[None, :]).astype(jnp.float32)
        l_own = (jnp.einsum('bhqd,bhkd->bhqk', qc, k_own).astype(jnp.float32) * scale
                 - slopes[None, :, None, None] * d_own[None, None])
        l_own = jnp.where((d_own >= 0)[None, None], l_own, -jnp.inf)
        n_k = n_sel * MOBA_BLOCK
        logits = jnp.concatenate([l_sel.reshape(B, H, MOBA_QCHUNK, n_k), l_own], axis=-1)
        p = jax.nn.softmax(logits, axis=-1).astype(v.dtype)
        p_sel = p[..., :n_k].reshape(B, H, MOBA_QCHUNK, n_sel, MOBA_BLOCK)
        p_own = p[..., n_k:]
        return (jnp.einsum('bhqnk,bhqnkd->bhqd', p_sel, v_sel)
                + jnp.einsum('bhqk,bhkd->bhqd', p_own, v_own))

    out = lax.map(chunk, jnp.arange(n_chunks))
    return out.transpose(1, 2, 0, 3, 4).reshape(B, H, S, D)


def gla_mixer(q, k, v, g_out, a_lr, w_a, b_a, out_gain):
    B, S, H, DK = q.shape
    DV = v.shape[-1]
    n = S // GLA_CHUNK
    C = GLA_CHUNK
    log_a = jax.nn.log_sigmoid((a_lr @ w_a + b_a).astype(jnp.float32)) / GLA_TAU

    def to_chunks(t, d):
        return t.astype(jnp.float32).reshape(B, n, C, H, d).transpose(0, 3, 1, 2, 4)

    qc = to_chunks(q, DK) * (DK ** -0.5)
    kc = to_chunks(k, DK)
    vc = to_chunks(v, DV)
    lam = jnp.cumsum(to_chunks(log_a, DK), axis=3)
    lam_last = lam[:, :, :, -1:, :]
    qe = qc * jnp.exp(lam)
    ke = kc * jnp.exp(-lam)
    kd = kc * jnp.exp(lam_last - lam)
    causal = jnp.tril(jnp.ones((C, C), dtype=bool))
    att = jnp.where(causal, jnp.einsum('bhncd,bhnsd->bhncs', qe, ke), 0.0)
    o_intra = jnp.einsum('bhncs,bhnsv->bhncv', att, vc)

    def step(state, inp):
        qe_n, kd_n, v_n, decay_n = inp
        o = jnp.einsum('bhcd,bhdv->bhcv', qe_n, state)
        state = state * decay_n[..., None] + jnp.einsum('bhcd,bhcv->bhdv', kd_n, v_n)
        return state, o

    xs = (jnp.moveaxis(qe, 2, 0), jnp.moveaxis(kd, 2, 0), jnp.moveaxis(vc, 2, 0),
          jnp.moveaxis(jnp.exp(lam_last[:, :, :, 0, :]), 2, 0))
    state0 = jnp.zeros((B, H, DK, DV), jnp.float32)
    _, o_inter = lax.scan(step, state0, xs)
    o = o_intra + jnp.moveaxis(o_inter, 0, 2)
    o = o.transpose(0, 2, 3, 1, 4).reshape(B, S, H, DV)
    o = rms_norm(o, out_gain) * jax.nn.silu(g_out.astype(jnp.float32))
    return o.astype(v.dtype).reshape(B, S, H * DV)


def swa_attention(q, k, v, slopes, sinks):
    B, Hq, S, D = q.shape
    Hkv = k.shape[1]
    G = Hq // Hkv
    W = SWA_WINDOW
    nb = S // W
    qb = q.reshape(B, Hkv, G, nb, W, D)
    kb = k.reshape(B, Hkv, nb, W, D)
    vb = v.reshape(B, Hkv, nb, W, D)
    kprev = jnp.pad(kb, ((0, 0), (0, 0), (1, 0), (0, 0), (0, 0)))[:, :, :-1]
    vprev = jnp.pad(vb, ((0, 0), (0, 0), (1, 0), (0, 0), (0, 0)))[:, :, :-1]
    kw = jnp.concatenate([kprev, kb], axis=3)
    vw = jnp.concatenate([vprev, vb], axis=3)
    i = jnp.arange(W)[:, None]
    j = jnp.arange(2 * W)[None, :]
    dist = W + i - j
    band = (dist >= 0) & (dist < W)
    has_prev = (jnp.arange(nb) > 0)[:, None, None] | (j >= W)[None]
    mask = band[None] & has_prev
    sl = slopes.reshape(Hkv, G)[None, :, :, None, None, None]
    logits = (jnp.einsum('bkgnqd,bknsd->bkgnqs', qb, kw).astype(jnp.float32) * (D ** -0.5)
              - sl * dist.astype(jnp.float32))
    logits = jnp.where(mask, logits, -jnp.inf)
    sink = jnp.broadcast_to(sinks.astype(jnp.float32).reshape(Hkv, G)[None, :, :, None, None, None],
                            (B, Hkv, G, nb, W, 1))
    p = jax.nn.softmax(jnp.concatenate([logits, sink], axis=-1), axis=-1)[..., :2 * W]
    out = jnp.einsum('bkgnqs,bknsd->bkgnqd', p.astype(v.dtype), vw)
    return out.reshape(B, Hq, S, D)


def setup_inputs(seed: int = 0) -> dict:
    key = jax.random.key(seed)
    ks = jax.random.split(key, 18)

    def nrm(k, shape, scale):
        return jax.random.normal(k, shape, jnp.float32) * scale

    def gain(k, shape):
        return 1.0 + 0.1 * jax.random.normal(k, shape, jnp.float32)

    return {
        "x": nrm(ks[0], (BATCH, SEQ, D_MODEL), 1.0),
        "c": nrm(ks[1], (BATCH, D_MODEL), 1.0),
        "w_ada": nrm(ks[2], (DEPTH, D_MODEL, 6 * D_MODEL), 0.5 * D_MODEL ** -0.5),
        "b_ada": nrm(ks[3], (DEPTH, 6 * D_MODEL), 0.02),
        "norm_attn": gain(ks[4], (DEPTH, D_MODEL)),
        "w_in": nrm(ks[5], (DEPTH, D_MODEL, D_IN), D_MODEL ** -0.5),
        "moba_q_norm": gain(ks[6], (DEPTH, HEAD_DIM)),
        "moba_k_norm": gain(ks[7], (DEPTH, HEAD_DIM)),
        "gla_w_a": nrm(ks[8], (DEPTH, GLA_LOWRANK, GLA_HEADS * GLA_DK), GLA_LOWRANK ** -0.5),
        "gla_b_a": nrm(ks[9], (DEPTH, GLA_HEADS * GLA_DK), 0.02),
        "gla_out_norm": gain(ks[10], (DEPTH, GLA_DV)),
        "swa_q_norm": gain(ks[11], (DEPTH, HEAD_DIM)),
        "swa_k_norm": gain(ks[12], (DEPTH, HEAD_DIM)),
        "swa_sinks": nrm(ks[13], (DEPTH, SWA_HEADS), 0.5),
        "w_out": nrm(ks[14], (DEPTH, MIX_WIDTH, D_MODEL), MIX_WIDTH ** -0.5),
        "norm_mlp": gain(ks[15], (DEPTH, D_MODEL)),
        "w_mlp_in": nrm(ks[16], (DEPTH, D_MODEL, D_FF), D_MODEL ** -0.5),
        "w_mlp_out": nrm(ks[17], (DEPTH, D_FF, D_MODEL), D_FF ** -0.5),
    }


def reference(x, c, w_ada, b_ada, norm_attn, w_in, moba_q_norm, moba_k_norm, gla_w_a, gla_b_a,
              gla_out_norm, swa_q_norm, swa_k_norm, swa_sinks, w_out, norm_mlp, w_mlp_in, w_mlp_out):
    B, S, _ = x.shape
    bounds = np.cumsum(SPLIT_SIZES)[:-1].tolist()
    moba_slopes = alibi_slopes(MOBA_HEADS)
    swa_slopes = alibi_slopes(SWA_HEADS)
    cond = jax.nn.silu(c)
    for l in range(DEPTH):
        mod = (cond @ w_ada[l] + b_ada[l])[:, None, :]
        sh_a, sc_a, g_a, sh_m, sc_m, g_m = jnp.split(mod, 6, axis=-1)
        h = rms_norm(x, norm_attn[l]) * (1.0 + sc_a) + sh_a
        proj = h @ w_in[l]
        mq, mk, mv, gq, gk, gv, gg, ga, sq, sk, sv = jnp.split(proj, bounds, axis=-1)
        mq = rms_norm(mq.reshape(B, S, MOBA_HEADS, HEAD_DIM), moba_q_norm[l]).transpose(0, 2, 1, 3)
        mk = rms_norm(mk.reshape(B, S, MOBA_HEADS, HEAD_DIM), moba_k_norm[l]).transpose(0, 2, 1, 3)
        mv = mv.reshape(B, S, MOBA_HEADS, HEAD_DIM).transpose(0, 2, 1, 3)
        y_moba = moba_attention(mq, mk, mv, moba_slopes).transpose(0, 2, 1, 3).reshape(B, S, -1)
        y_gla = gla_mixer(gq.reshape(B, S, GLA_HEADS, GLA_DK), gk.reshape(B, S, GLA_HEADS, GLA_DK),
                          gv.reshape(B, S, GLA_HEADS, GLA_DV), gg.reshape(B, S, GLA_HEADS, GLA_DV),
                          ga, gla_w_a[l], gla_b_a[l], gla_out_norm[l])
        sq = rms_norm(sq.reshape(B, S, SWA_HEADS, HEAD_DIM), swa_q_norm[l]).transpose(0, 2, 1, 3)
        sk = rms_norm(sk.reshape(B, S, SWA_KV_HEADS, HEAD_DIM), swa_k_norm[l]).transpose(0, 2, 1, 3)
        sv = sv.reshape(B, S, SWA_KV_HEADS, HEAD_DIM).transpose(0, 2, 1, 3)
        y_swa = swa_attention(sq, sk, sv, swa_slopes, swa_sinks[l]).transpose(0, 2, 1, 3).reshape(B, S, -1)
        mix = jnp.concatenate([y_moba, y_gla, y_swa], axis=-1)
        x = x + g_a * (mix @ w_out[l])
        h = rms_norm(x, norm_mlp[l]) * (1.0 + sc_m) + sh_m
        x = x + g_m * (jnp.square(jax.nn.relu(h @ w_mlp_in[l])) @ w_mlp_out[l])
    return x
```

```python
import functools

import jax
import jax.numpy as jnp
from jax import lax
from jax.experimental import pallas as pl
from jax.experimental.pallas import tpu as pltpu

HEAD_DIM = 128
MOBA_BLOCK = 256
MOBA_TOPK = 3
GLA_DV = 128
GLA_DK = 64
GLA_LOWRANK = 16
GLA_TAU = 16.0
GLA_CHUNK = 64
SWA_KV_HEADS = 4
SWA_WINDOW = 128
ALIBI_MAX_EXP = 8.0
EPS = 1e-6

LANES = 128
VMEM_LIMIT_BYTES = 56 * 1024 * 1024

_F32 = jnp.float32
_BF16 = jnp.bfloat16
_NT = (((1,), (1,)), ((), ()))
_TN = (((0,), (0,)), ((), ()))


def _params(*sem):
    return pltpu.CompilerParams(dimension_semantics=sem, vmem_limit_bytes=VMEM_LIMIT_BYTES)


def _rms(x, gain):
    return x * lax.rsqrt(jnp.mean(x * x, axis=-1, keepdims=True) + EPS) * gain


def _ada_kernel(c_ref, w_ref, b_ref, o_ref):
    c = c_ref[...]
    cond = (c * jax.nn.sigmoid(c)).astype(_BF16)
    acc = jnp.dot(cond, w_ref[0].astype(_BF16), preferred_element_type=_F32)
    o_ref[0] = acc + b_ref[0]


def _ada(c_pad, w_ada, b_ada, *, tn=1024):
    depth, d, n = w_ada.shape
    rows = c_pad.shape[0]
    return pl.pallas_call(
        _ada_kernel,
        grid=(depth, n // tn),
        in_specs=[pl.BlockSpec((rows, d), lambda l, j: (0, 0)),
                  pl.BlockSpec((1, d, tn), lambda l, j: (l, 0, j)),
                  pl.BlockSpec((1, 1, tn), lambda l, j: (l, 0, j))],
        out_specs=pl.BlockSpec((1, rows, tn), lambda l, j: (l, 0, j)),
        out_shape=jax.ShapeDtypeStruct((depth, rows, n), _F32),
        compiler_params=_params("parallel", "parallel"),
        name="ada_mod",
    )(c_pad, w_ada, b_ada.reshape(depth, 1, n))


def _norm_mod_kernel(x_ref, gain_ref, sc_ref, sh_ref, o_ref):
    x = x_ref[0]
    inv = lax.rsqrt(jnp.mean(x * x, axis=-1, keepdims=True) + EPS)
    y = x * inv * gain_ref[...]
    o_ref[0] = (y * (1.0 + sc_ref[0]) + sh_ref[0]).astype(o_ref.dtype)


def _norm_mod(x, gain, mod, sc_idx, sh_idx, *, ts=256):
    b, s, d = x.shape
    return pl.pallas_call(
        _norm_mod_kernel,
        grid=(b, s // ts),
        in_specs=[pl.BlockSpec((1, ts, d), lambda i, t: (i, t, 0)),
                  pl.BlockSpec((1, d), lambda i, t: (0, 0)),
                  pl.BlockSpec((1, 1, d), lambda i, t: (i * 6 + sc_idx, 0, 0)),
                  pl.BlockSpec((1, 1, d), lambda i, t: (i * 6 + sh_idx, 0, 0))],
        out_specs=pl.BlockSpec((1, ts, d), lambda i, t: (i, t, 0)),
        out_shape=jax.ShapeDtypeStruct((b, s, d), _BF16),
        compiler_params=_params("parallel", "parallel"),
        name="norm_mod",
    )(x, gain.reshape(1, d), mod, mod)


def _mm_kernel(x_ref, w_ref, o_ref, *, relu2):
    acc = jnp.dot(x_ref[...], w_ref[...], preferred_element_type=_F32)
    if relu2:
        acc = jnp.square(jnp.maximum(acc, 0.0))
    o_ref[...] = acc.astype(o_ref.dtype)


def _matmul(x, w, *, tm, tn, out_dtype, relu2=False, name):
    m, k = x.shape
    n = w.shape[1]
    return pl.pallas_call(
        functools.partial(_mm_kernel, relu2=relu2),
        grid=(m // tm, n // tn),
        in_specs=[pl.BlockSpec((tm, k), lambda i, j: (i, 0)),
                  pl.BlockSpec((k, tn), lambda i, j: (0, j))],
        out_specs=pl.BlockSpec((tm, tn), lambda i, j: (i, j)),
        out_shape=jax.ShapeDtypeStruct((m, n), out_dtype),
        compiler_params=_params("parallel", "parallel"),
        name=name,
    )(x, w)


def _outproj_kernel(ym_ref, yg_ref, ys_ref, w_ref, x_ref, g_ref, o_ref):
    km = ym_ref.shape[-1]
    kg = yg_ref.shape[-1]
    acc = jnp.dot(ym_ref[0], w_ref[0:km, :], preferred_element_type=_F32)
    acc += jnp.dot(yg_ref[0], w_ref[km:km + kg, :], preferred_element_type=_F32)
    acc += jnp.dot(ys_ref[0], w_ref[km + kg:, :], preferred_element_type=_F32)
    o_ref[0] = x_ref[0] + g_ref[0] * acc


def _outproj(y_moba, y_gla, y_swa, w, x, mod, gate_idx, *, tm=1024, tn=1024):
    b, s, d = x.shape
    k = w.shape[0]
    km, kg, ks = y_moba.shape[-1], y_gla.shape[-1], y_swa.shape[-1]
    return pl.pallas_call(
        _outproj_kernel,
        grid=(b, s // tm, d // tn),
        in_specs=[pl.BlockSpec((1, tm, km), lambda i, t, j: (i, t, 0)),
                  pl.BlockSpec((1, tm, kg), lambda i, t, j: (i, t, 0)),
                  pl.BlockSpec((1, tm, ks), lambda i, t, j: (i, t, 0)),
                  pl.BlockSpec((k, tn), lambda i, t, j: (0, j)),
                  pl.BlockSpec((1, tm, tn), lambda i, t, j: (i, t, j)),
                  pl.BlockSpec((1, 1, tn), lambda i, t, j: (i * 6 + gate_idx, 0, j))],
        out_specs=pl.BlockSpec((1, tm, tn), lambda i, t, j: (i, t, j)),
        out_shape=jax.ShapeDtypeStruct((b, s, d), _F32),
        compiler_params=_params("parallel", "parallel", "parallel"),
        name="out_proj",
    )(y_moba, y_gla, y_swa, w, x, mod)


def _mlp_out_kernel(h_ref, w_ref, x_ref, g_ref, o_ref, *, nk):
    kk = pl.program_id(3)
    part = jnp.dot(h_ref[0], w_ref[...], preferred_element_type=_F32)

    @pl.when(kk == 0)
    def _():
        o_ref[0] = part

    @pl.when(kk > 0)
    def _():
        o_ref[0] += part

    @pl.when(kk == nk - 1)
    def _():
        o_ref[0] = x_ref[0] + g_ref[0] * o_ref[0]


def _mlp_out(h, w, x, mod, gate_idx, *, tm=1024, tn=1024, tk=2048):
    b, s, d = x.shape
    f = w.shape[0]
    nk = f // tk
    return pl.pallas_call(
        functools.partial(_mlp_out_kernel, nk=nk),
        grid=(b, s // tm, d // tn, nk),
        in_specs=[pl.BlockSpec((1, tm, tk), lambda i, t, j, k: (i, t, k)),
                  pl.BlockSpec((tk, tn), lambda i, t, j, k: (k, j)),
                  pl.BlockSpec((1, tm, tn), lambda i, t, j, k: (i, t, j)),
                  pl.BlockSpec((1, 1, tn), lambda i, t, j, k: (i * 6 + gate_idx, 0, j))],
        out_specs=pl.BlockSpec((1, tm, tn), lambda i, t, j, k: (i, t, j)),
        out_shape=jax.ShapeDtypeStruct((b, s, d), _F32),
        compiler_params=_params("parallel", "parallel", "parallel", "arbitrary"),
        name="mlp_out",
    )(h, w, x, mod)


def _moba_kernel(q_ref, k_ref, v_ref, qg_ref, kg_ref, slope_ref, o_ref, *, blk, topk):
    s_len, d = q_ref.shape[1], q_ref.shape[2]
    nb = s_len // blk
    q = _rms(q_ref[0], qg_ref[...])
    k = _rms(k_ref[0], kg_ref[...])
    kmean = jnp.sum(k.reshape(nb, blk, d), axis=1) * (1.0 / blk)
    kmean = jnp.concatenate([kmean, jnp.zeros((LANES - nb, d), _F32)], axis=0)
    gate = lax.dot_general(q, kmean, _NT, precision=lax.Precision.HIGHEST,
                           preferred_element_type=_F32)
    qb = q.astype(_BF16)
    kb = k.astype(_BF16)
    vb = v_ref[0].astype(_BF16)
    slope = slope_ref[0][:, 0:1]
    scale = d ** -0.5
    row = lax.broadcasted_iota(jnp.int32, (blk, blk), 0)
    col = lax.broadcasted_iota(jnp.int32, (blk, blk), 1)
    rc = (row - col).astype(_F32)
    lane = lax.broadcasted_iota(jnp.int32, (blk, LANES), 1)
    neg_inf = -jnp.inf

    for i in range(nb):
        qi = qb[i * blk:(i + 1) * blk]
        n_keys = (i + 1) * blk
        s = lax.dot_general(qi, kb[:n_keys], _NT, preferred_element_type=_F32)
        sel = None
        if i > topk:
            gi = gate[i * blk:(i + 1) * blk]
            rank = jnp.zeros((blk, LANES), jnp.int32)
            for jp in range(i):
                cj = gi[:, jp:jp + 1]
                beats = (cj > gi) | ((cj == gi) & (lane > jp))
                rank += beats.astype(jnp.int32)
            sel = rank < topk
        parts = []
        for j in range(i + 1):
            dist = rc + float((i - j) * blk)
            lj = s[:, j * blk:(j + 1) * blk] * scale - slope * dist
            if j == i:
                lj = jnp.where(rc >= 0, lj, neg_inf)
            elif sel is not None:
                lj = jnp.where(sel[:, j:j + 1], lj, neg_inf)
            parts.append(lj)
        m = parts[0].max(axis=-1, keepdims=True)
        for lj in parts[1:]:
            m = jnp.maximum(m, lj.max(axis=-1, keepdims=True))
        ps = [jnp.exp(lj - m) for lj in parts]
        den = ps[0].sum(axis=-1, keepdims=True)
        for pj in ps[1:]:
            den += pj.sum(axis=-1, keepdims=True)
        p = jnp.concatenate([pj.astype(_BF16) for pj in ps], axis=-1)
        pv = jnp.dot(p, vb[:n_keys], preferred_element_type=_F32)
        o_ref[0, i * blk:(i + 1) * blk, :] = (pv / den).astype(o_ref.dtype)


def _moba(proj, q_gain, k_gain, slopes, *, heads, q_off, k_off, v_off):
    b, s, _ = proj.shape
    d = HEAD_DIM
    blk_spec = lambda off: pl.BlockSpec((1, s, d), lambda i, h: (i, 0, off + h))
    return pl.pallas_call(
        functools.partial(_moba_kernel, blk=MOBA_BLOCK, topk=MOBA_TOPK),
        grid=(b, heads),
        in_specs=[blk_spec(q_off), blk_spec(k_off), blk_spec(v_off),
                  pl.BlockSpec((1, d), lambda i, h: (0, 0)),
                  pl.BlockSpec((1, d), lambda i, h: (0, 0)),
                  pl.BlockSpec((1, 1, LANES), lambda i, h: (h, 0, 0))],
        out_specs=pl.BlockSpec((1, s, d), lambda i, h: (i, 0, h)),
        out_shape=jax.ShapeDtypeStruct((b, s, heads * d), _BF16),
        compiler_params=_params("parallel", "parallel"),
        name="moba_attn",
    )(proj, proj, proj, q_gain.reshape(1, d), k_gain.reshape(1, d), slopes)


def _swa_kernel(q_ref, k_ref, v_ref, qg_ref, kg_ref, slope_ref, sink_ref, o_ref,
                qbuf, kbuf, vbuf, *, win, groups):
    s_len, d = k_ref.shape[1], k_ref.shape[2]
    kbuf[0:win] = jnp.zeros((win, d), _BF16)
    vbuf[0:win] = jnp.zeros((win, d), _BF16)
    kbuf[win:] = _rms(k_ref[0], kg_ref[...]).astype(_BF16)
    vbuf[win:] = v_ref[0].astype(_BF16)
    for g in range(groups):
        qbuf[g] = _rms(q_ref[0, :, g * d:(g + 1) * d], qg_ref[...]).astype(_BF16)

    rows = groups * win
    ri = lax.broadcasted_iota(jnp.int32, (rows, 2 * win), 0) % win
    ci = lax.broadcasted_iota(jnp.int32, (rows, 2 * win), 1)
    dist_i = win + ri - ci
    dist = dist_i.astype(_F32)
    band = (dist_i >= 0) & (dist_i < win)
    cur = ci >= win
    slope_col = jnp.concatenate(
        [jnp.broadcast_to(slope_ref[0][g:g + 1, 0:1], (win, 1)) for g in range(groups)], axis=0)
    sink_col = jnp.concatenate(
        [jnp.broadcast_to(sink_ref[0][g:g + 1, 0:1], (win, 1)) for g in range(groups)], axis=0)
    alibi = slope_col * dist
    scale = d ** -0.5

    def body(n, carry):
        st = pl.multiple_of(n * win, win)
        qs = jnp.concatenate([qbuf[g, pl.ds(st, win), :] for g in range(groups)], axis=0)
        kw = kbuf[pl.ds(st, 2 * win), :]
        vw = vbuf[pl.ds(st, 2 * win), :]
        s = lax.dot_general(qs, kw, _NT, preferred_element_type=_F32)
        mask = band & (cur | (n > 0))
        l = jnp.where(mask, s * scale - alibi, -jnp.inf)
        m = jnp.maximum(l.max(axis=-1, keepdims=True), sink_col)
        p = jnp.exp(l - m)
        den = p.sum(axis=-1, keepdims=True) + jnp.exp(sink_col - m)
        o = jnp.dot(p.astype(_BF16), vw, preferred_element_type=_F32) / den
        for g in range(groups):
            o_ref[0, pl.ds(st, win), g * d:(g + 1) * d] = o[g * win:(g + 1) * win].astype(o_ref.dtype)
        return carry

    lax.fori_loop(0, s_len // win, body, 0)


def _swa(proj, q_gain, k_gain, slopes, sinks, *, kv_heads, groups, q_off, k_off, v_off):
    b, s, _ = proj.shape
    d = HEAD_DIM
    win = SWA_WINDOW
    return pl.pallas_call(
        functools.partial(_swa_kernel, win=win, groups=groups),
        grid=(b, kv_heads),
        in_specs=[pl.BlockSpec((1, s, groups * d), lambda i, h: (i, 0, q_off + h)),
                  pl.BlockSpec((1, s, d), lambda i, h: (i, 0, k_off + h)),
                  pl.BlockSpec((1, s, d), lambda i, h: (i, 0, v_off + h)),
                  pl.BlockSpec((1, d), lambda i, h: (0, 0)),
                  pl.BlockSpec((1, d), lambda i, h: (0, 0)),
                  pl.BlockSpec((1, groups, LANES), lambda i, h: (h, 0, 0)),
                  pl.BlockSpec((1, groups, LANES), lambda i, h: (h, 0, 0))],
        out_specs=pl.BlockSpec((1, s, groups * d), lambda i, h: (i, 0, h)),
        out_shape=jax.ShapeDtypeStruct((b, s, kv_heads * groups * d), _BF16),
        scratch_shapes=[pltpu.VMEM((groups, s, d), _BF16),
                        pltpu.VMEM((s + win, d), _BF16),
                        pltpu.VMEM((s + win, d), _BF16)],
        compiler_params=_params("parallel", "parallel"),
        name="swa_attn",
    )(proj, proj, proj, q_gain.reshape(1, d), k_gain.reshape(1, d), slopes, sinks)


def _gla_kernel(q_ref, k_ref, v_ref, g_ref, a_ref, wa_ref, ba_ref, gn_ref, o_ref,
                qe_s, ke_s, kd_s, dec_s, o_s, *, chunk, dk, dv):
    s_len = q_ref.shape[1]
    n_chunks = s_len // chunk
    pair_k = q_ref.shape[2]
    pair_v = v_ref.shape[2]
    hi = lax.Precision.HIGHEST

    z = jnp.dot(a_ref[0], wa_ref[...], precision=hi, preferred_element_type=_F32) + ba_ref[...]
    log_a = (jnp.minimum(z, 0.0) - jnp.log1p(jnp.exp(-jnp.abs(z)))) * (1.0 / GLA_TAU)
    rin = lax.broadcasted_iota(jnp.int32, (s_len, pair_k), 0) % chunk
    lam = log_a
    shift = 1
    while shift < chunk:
        lam = lam + jnp.where(rin >= shift, pltpu.roll(lam, shift, 0), 0.0)
        shift *= 2
    lam3 = lam.reshape(n_chunks, chunk, pair_k)
    last3 = lam3[:, chunk - 1:chunk, :]
    lam_last = jnp.broadcast_to(last3, (n_chunks, chunk, pair_k)).reshape(s_len, pair_k)
    kk = k_ref[0]
    qe_s[...] = q_ref[0] * (dk ** -0.5) * jnp.exp(lam)
    ke_s[...] = kk * jnp.exp(-lam)
    kd_s[...] = kk * jnp.exp(lam_last - lam)
    dec_s[...] = jnp.exp(last3)

    lane_k = lax.broadcasted_iota(jnp.int32, (chunk, pair_k), 1)
    head0 = lane_k < dk
    tril = (lax.broadcasted_iota(jnp.int32, (chunk, chunk), 0)
            >= lax.broadcasted_iota(jnp.int32, (chunk, chunk), 1))
    srow = lax.broadcasted_iota(jnp.int32, (pair_v, pair_k), 0)
    scol = lax.broadcasted_iota(jnp.int32, (pair_v, pair_k), 1)
    same_head = (srow < dv) == (scol < dk)

    def body(n, state):
        r0 = pl.multiple_of(n * chunk, chunk)
        qe = qe_s[pl.ds(r0, chunk), :]
        ke = ke_s[pl.ds(r0, chunk), :]
        kd = kd_s[pl.ds(r0, chunk), :]
        vn = v_ref[0, pl.ds(r0, chunk), :]
        att0 = lax.dot_general(jnp.where(head0, qe, 0.0), ke, _NT, precision=hi,
                               preferred_element_type=_F32)
        att1 = lax.dot_general(jnp.where(head0, 0.0, qe), ke, _NT, precision=hi,
                               preferred_element_type=_F32)
        o0 = jnp.dot(jnp.where(tril, att0, 0.0), vn[:, :dv], precision=hi,
                     preferred_element_type=_F32)
        o1 = jnp.dot(jnp.where(tril, att1, 0.0), vn[:, dv:], precision=hi,
                     preferred_element_type=_F32)
        o_inter = lax.dot_general(qe, state, _NT, precision=hi, preferred_element_type=_F32)
        o_s[pl.ds(r0, chunk), :] = jnp.concatenate([o0, o1], axis=1) + o_inter
        upd = lax.dot_general(vn, kd, _TN, precision=hi, preferred_element_type=_F32)
        return state * dec_s[n] + jnp.where(same_head, upd, 0.0)

    lax.fori_loop(0, n_chunks, body, jnp.zeros((pair_v, pair_k), _F32))

    gate = g_ref[0]
    for h in range(pair_v // dv):
        o = _rms(o_s[:, h * dv:(h + 1) * dv], gn_ref[...])
        gh = gate[:, h * dv:(h + 1) * dv]
        o_ref[0, :, h * dv:(h + 1) * dv] = (o * (gh * jax.nn.sigmoid(gh))).astype(o_ref.dtype)


def _gla(proj, ga, w_a_pad, b_a, out_gain, *, heads, q_off, k_off, v_off, g_off):
    b, s, _ = proj.shape
    pairs = heads // 2
    pk, pv = 2 * GLA_DK, 2 * GLA_DV
    n_chunks = s // GLA_CHUNK
    return pl.pallas_call(
        functools.partial(_gla_kernel, chunk=GLA_CHUNK, dk=GLA_DK, dv=GLA_DV),
        grid=(b, pairs),
        in_specs=[pl.BlockSpec((1, s, pk), lambda i, p: (i, 0, q_off + p)),
                  pl.BlockSpec((1, s, pk), lambda i, p: (i, 0, k_off + p)),
                  pl.BlockSpec((1, s, pv), lambda i, p: (i, 0, v_off + p)),
                  pl.BlockSpec((1, s, pv), lambda i, p: (i, 0, g_off + p)),
                  pl.BlockSpec((1, s, LANES), lambda i, p: (i, 0, 0)),
                  pl.BlockSpec((LANES, pk), lambda i, p: (0, p)),
                  pl.BlockSpec((1, pk), lambda i, p: (0, p)),
                  pl.BlockSpec((1, GLA_DV), lambda i, p: (0, 0))],
        out_specs=pl.BlockSpec((1, s, pv), lambda i, p: (i, 0, p)),
        out_shape=jax.ShapeDtypeStruct((b, s, heads * GLA_DV), _BF16),
        scratch_shapes=[pltpu.VMEM((s, pk), _F32), pltpu.VMEM((s, pk), _F32),
                        pltpu.VMEM((s, pk), _F32), pltpu.VMEM((n_chunks, 1, pk), _F32),
                        pltpu.VMEM((s, pv), _F32)],
        compiler_params=_params("parallel", "parallel"),
        name="gla_mixer",
    )(proj, proj, proj, proj, ga, w_a_pad, b_a.reshape(1, -1), out_gain.reshape(1, -1))


def _alibi_slopes(n_heads):
    return jnp.exp2(-ALIBI_MAX_EXP * jnp.arange(1, n_heads + 1, dtype=_F32) / n_heads)


def _lane_rows(vec, lead):
    return jnp.broadcast_to(vec.astype(_F32)[:, None], (vec.shape[0], LANES)).reshape(lead, -1, LANES)


def kernel(x, c, w_ada, b_ada, norm_attn, w_in, moba_q_norm, moba_k_norm, gla_w_a, gla_b_a,
           gla_out_norm, swa_q_norm, swa_k_norm, swa_sinks, w_out, norm_mlp, w_mlp_in, w_mlp_out):
    b, s, d = x.shape
    depth = w_ada.shape[0]
    d_ff = w_mlp_in.shape[-1]
    swa_heads = swa_sinks.shape[-1]
    gla_heads = gla_w_a.shape[-1] // GLA_DK
    moba_heads = (d - gla_heads * GLA_DV - swa_heads * HEAD_DIM) // HEAD_DIM
    groups = swa_heads // SWA_KV_HEADS
    n_moba = moba_heads * HEAD_DIM
    n_gla = gla_heads * (2 * GLA_DK + 2 * GLA_DV)
    lowrank_start = 3 * n_moba + n_gla
    lowrank_end = lowrank_start + GLA_LOWRANK

    q_m, k_m, v_m = 0, moba_heads, 2 * moba_heads
    gla0 = 3 * n_moba
    q_g = gla0 // (2 * GLA_DK)
    k_g = (gla0 + gla_heads * GLA_DK) // (2 * GLA_DK)
    v_g = (gla0 + 2 * gla_heads * GLA_DK) // (2 * GLA_DV)
    g_g = (gla0 + 2 * gla_heads * GLA_DK + gla_heads * GLA_DV) // (2 * GLA_DV)
    swa0 = gla0 + n_gla
    q_s = swa0 // (groups * HEAD_DIM)
    k_s = (swa0 + swa_heads * HEAD_DIM) // HEAD_DIM
    v_s = k_s + SWA_KV_HEADS

    moba_slopes = _lane_rows(_alibi_slopes(moba_heads), moba_heads)
    swa_slopes = _lane_rows(_alibi_slopes(swa_heads), SWA_KV_HEADS)

    c_pad = jnp.pad(c, ((0, 8 - b), (0, 0)))
    mod_all = _ada(c_pad, w_ada, b_ada)

    for l in range(depth):
        mod = mod_all[l, :b].reshape(b * 6, 1, d)
        w_main = jnp.concatenate([w_in[l][:, :lowrank_start], w_in[l][:, lowrank_end:]],
                                 axis=1).astype(_BF16)
        w_low = jnp.pad(w_in[l][:, lowrank_start:lowrank_end],
                        ((0, 0), (0, LANES - GLA_LOWRANK))).astype(_BF16)
        w_a_pad = jnp.pad(gla_w_a[l], ((0, LANES - GLA_LOWRANK), (0, 0)))

        h = _norm_mod(x, norm_attn[l], mod, 1, 0).reshape(b * s, d)
        proj = _matmul(h, w_main, tm=1024, tn=1024, out_dtype=_F32, name="in_proj")
        proj = proj.reshape(b, s, -1)
        ga = _matmul(h, w_low, tm=1024, tn=LANES, out_dtype=_F32, name="decay_proj")
        ga = ga.reshape(b, s, LANES)

        y_moba = _moba(proj, moba_q_norm[l], moba_k_norm[l], moba_slopes,
                       heads=moba_heads, q_off=q_m, k_off=k_m, v_off=v_m)
        y_gla = _gla(proj, ga, w_a_pad, gla_b_a[l], gla_out_norm[l],
                     heads=gla_heads, q_off=q_g, k_off=k_g, v_off=v_g, g_off=g_g)
        y_swa = _swa(proj, swa_q_norm[l], swa_k_norm[l], swa_slopes,
                     _lane_rows(swa_sinks[l], SWA_KV_HEADS),
                     kv_heads=SWA_KV_HEADS, groups=groups, q_off=q_s, k_off=k_s, v_off=v_s)
        x = _outproj(y_moba, y_gla, y_swa, w_out[l].astype(_BF16), x, mod, 2)

        h = _norm_mod(x, norm_mlp[l], mod, 4, 3).reshape(b * s, d)
        hid = _matmul(h, w_mlp_in[l].astype(_BF16), tm=1024, tn=1024, out_dtype=_BF16,
                      relu2=True, name="mlp_in")
        x = _mlp_out(hid.reshape(b, s, d_ff), w_mlp_out[l].astype(_BF16), x, mod, 5)
    return x
```

```python
import functools

import jax
import jax.numpy as jnp
from jax import lax
from jax.experimental import pallas as pl
from jax.experimental.pallas import tpu as pltpu

HEAD_DIM = 128
MOBA_BLOCK = 256
MOBA_TOPK = 3
GLA_DV = 128
GLA_DK = 64
GLA_LOWRANK = 16
GLA_TAU = 16.0
GLA_CHUNK = 64
SWA_KV_HEADS = 4
SWA_WINDOW = 128
ALIBI_MAX_EXP = 8.0
EPS = 1e-6

LANES = 128
VMEM_LIMIT_BYTES = 56 * 1024 * 1024

_F32 = jnp.float32
_BF16 = jnp.bfloat16
_NT = (((1,), (1,)), ((), ()))
_TN = (((0,), (0,)), ((), ()))


def _params(*sem):
    return pltpu.CompilerParams(dimension_semantics=sem, vmem_limit_bytes=VMEM_LIMIT_BYTES)


def _rms(x, gain):
    return x * lax.rsqrt(jnp.mean(x * x, axis=-1, keepdims=True) + EPS) * gain


def _ada_kernel(c_ref, w_ref, b_ref, o_ref):
    kk = pl.program_id(1)
    c = c_ref[...]
    cond = (c * jax.nn.sigmoid(c)).astype(_BF16)
    part = jnp.dot(cond, w_ref[0].astype(_BF16), preferred_element_type=_F32)

    @pl.when(kk == 0)
    def _():
        o_ref[0] = part + b_ref[0]

    @pl.when(kk > 0)
    def _():
        o_ref[0] += part


def _ada(c_pad, w_ada, b_ada, *, tk=128):
    depth, d, n = w_ada.shape
    rows = c_pad.shape[0]
    return pl.pallas_call(
        _ada_kernel,
        grid=(depth, d // tk),
        in_specs=[pl.BlockSpec((rows, tk), lambda l, k: (0, k)),
                  pl.BlockSpec((1, tk, n), lambda l, k: (l, k, 0)),
                  pl.BlockSpec((1, 1, n), lambda l, k: (l, 0, 0))],
        out_specs=pl.BlockSpec((1, rows, n), lambda l, k: (l, 0, 0)),
        out_shape=jax.ShapeDtypeStruct((depth, rows, n), _F32),
        compiler_params=_params("parallel", "arbitrary"),
        name="ada_mod",
    )(c_pad, w_ada, b_ada.reshape(depth, 1, n))


def _norm_mod_kernel(x_ref, gain_ref, sc_ref, sh_ref, o_ref):
    x = x_ref[0]
    inv = lax.rsqrt(jnp.mean(x * x, axis=-1, keepdims=True) + EPS)
    y = x * inv * gain_ref[...]
    o_ref[0] = (y * (1.0 + sc_ref[0]) + sh_ref[0]).astype(o_ref.dtype)


def _norm_mod(x, gain, mod, sc_idx, sh_idx, *, ts=256):
    b, s, d = x.shape
    return pl.pallas_call(
        _norm_mod_kernel,
        grid=(b, s // ts),
        in_specs=[pl.BlockSpec((1, ts, d), lambda i, t: (i, t, 0)),
                  pl.BlockSpec((1, d), lambda i, t: (0, 0)),
                  pl.BlockSpec((1, 1, d), lambda i, t: (i * 6 + sc_idx, 0, 0)),
                  pl.BlockSpec((1, 1, d), lambda i, t: (i * 6 + sh_idx, 0, 0))],
        out_specs=pl.BlockSpec((1, ts, d), lambda i, t: (i, t, 0)),
        out_shape=jax.ShapeDtypeStruct((b, s, d), _BF16),
        compiler_params=_params("parallel", "parallel"),
        name="norm_mod",
    )(x, gain.reshape(1, d), mod, mod)


def _mm_kernel(x_ref, w_ref, o_ref, *, relu2):
    acc = jnp.dot(x_ref[...], w_ref[0].astype(_BF16), preferred_element_type=_F32)
    if relu2:
        acc = jnp.square(jnp.maximum(acc, 0.0))
    o_ref[...] = acc.astype(o_ref.dtype)


def _matmul(x, w_all, layer, *, tm=1024, tn=512, out_dtype, relu2=False, name):
    m, k = x.shape
    n = w_all.shape[2]
    return pl.pallas_call(
        functools.partial(_mm_kernel, relu2=relu2),
        grid=(m // tm, pl.cdiv(n, tn)),
        in_specs=[pl.BlockSpec((tm, k), lambda i, j: (i, 0)),
                  pl.BlockSpec((1, k, tn), lambda i, j: (layer, 0, j))],
        out_specs=pl.BlockSpec((tm, tn), lambda i, j: (i, j)),
        out_shape=jax.ShapeDtypeStruct((m, n), out_dtype),
        compiler_params=_params("parallel", "parallel"),
        name=name,
    )(x, w_all)


def _outproj_kernel(ym_ref, yg_ref, ys_ref, w_ref, x_ref, g_ref, o_ref):
    km = ym_ref.shape[-1]
    kg = yg_ref.shape[-1]
    w = w_ref[0].astype(_BF16)
    acc = jnp.dot(ym_ref[0], w[0:km, :], preferred_element_type=_F32)
    acc += jnp.dot(yg_ref[0], w[km:km + kg, :], preferred_element_type=_F32)
    acc += jnp.dot(ys_ref[0], w[km + kg:, :], preferred_element_type=_F32)
    o_ref[0] = x_ref[0] + g_ref[0] * acc


def _outproj(y_moba, y_gla, y_swa, w_all, layer, x, mod, gate_idx, *, tm=1024, tn=512):
    b, s, d = x.shape
    k = w_all.shape[1]
    km, kg, ks = y_moba.shape[-1], y_gla.shape[-1], y_swa.shape[-1]
    return pl.pallas_call(
        _outproj_kernel,
        grid=(b, s // tm, d // tn),
        in_specs=[pl.BlockSpec((1, tm, km), lambda i, t, j: (i, t, 0)),
                  pl.BlockSpec((1, tm, kg), lambda i, t, j: (i, t, 0)),
                  pl.BlockSpec((1, tm, ks), lambda i, t, j: (i, t, 0)),
                  pl.BlockSpec((1, k, tn), lambda i, t, j: (layer, 0, j)),
                  pl.BlockSpec((1, tm, tn), lambda i, t, j: (i, t, j)),
                  pl.BlockSpec((1, 1, tn), lambda i, t, j: (i * 6 + gate_idx, 0, j))],
        out_specs=pl.BlockSpec((1, tm, tn), lambda i, t, j: (i, t, j)),
        out_shape=jax.ShapeDtypeStruct((b, s, d), _F32),
        compiler_params=_params("parallel", "parallel", "parallel"),
        name="out_proj",
    )(y_moba, y_gla, y_swa, w_all, x, mod)


def _mlp_out_kernel(h_ref, w_ref, x_ref, g_ref, o_ref, *, nk):
    kk = pl.program_id(3)
    part = jnp.dot(h_ref[0], w_ref[0].astype(_BF16), preferred_element_type=_F32)

    @pl.when(kk == 0)
    def _():
        o_ref[0] = part

    @pl.when((kk > 0) & (kk < nk - 1))
    def _():
        o_ref[0] += part

    @pl.when(kk == nk - 1)
    def _():
        o_ref[0] = x_ref[0] + g_ref[0] * (o_ref[0] + part)


def _mlp_out(h, w_all, layer, x, mod, gate_idx, *, tm=1024, tn=512, tk=4096):
    b, s, d = x.shape
    f = w_all.shape[1]
    nk = f // tk
    assert nk >= 2
    return pl.pallas_call(
        functools.partial(_mlp_out_kernel, nk=nk),
        grid=(b, s // tm, d // tn, nk),
        in_specs=[pl.BlockSpec((1, tm, tk), lambda i, t, j, k: (i, t, k)),
                  pl.BlockSpec((1, tk, tn), lambda i, t, j, k: (layer, k, j)),
                  pl.BlockSpec((1, tm, tn), lambda i, t, j, k: (i, t, j)),
                  pl.BlockSpec((1, 1, tn), lambda i, t, j, k: (i * 6 + gate_idx, 0, j))],
        out_specs=pl.BlockSpec((1, tm, tn), lambda i, t, j, k: (i, t, j)),
        out_shape=jax.ShapeDtypeStruct((b, s, d), _F32),
        compiler_params=_params("parallel", "parallel", "parallel", "arbitrary"),
        name="mlp_out",
    )(h, w_all, x, mod)


def _moba_kernel(q_ref, k_ref, v_ref, qg_ref, kg_ref, slope_ref, o_ref, *, blk, topk):
    s_len, d = q_ref.shape[1], q_ref.shape[2]
    nb = s_len // blk
    q = _rms(q_ref[0], qg_ref[...])
    k = _rms(k_ref[0], kg_ref[...])
    kmean = jnp.sum(k.reshape(nb, blk, d), axis=1) * (1.0 / blk)
    kmean = jnp.concatenate([kmean, jnp.zeros((LANES - nb, d), _F32)], axis=0)
    gate = lax.dot_general(q, kmean, _NT, precision=lax.Precision.HIGHEST,
                           preferred_element_type=_F32)
    qb = q.astype(_BF16)
    kb = k.astype(_BF16)
    vb = v_ref[0].astype(_BF16)
    slope = slope_ref[0][:, 0:1]
    scale = d ** -0.5
    row = lax.broadcasted_iota(jnp.int32, (blk, blk), 0)
    col = lax.broadcasted_iota(jnp.int32, (blk, blk), 1)
    rc = (row - col).astype(_F32)
    lane = lax.broadcasted_iota(jnp.int32, (blk, LANES), 1)
    neg_inf = -jnp.inf

    for i in range(nb):
        qi = qb[i * blk:(i + 1) * blk]
        n_keys = (i + 1) * blk
        s = lax.dot_general(qi, kb[:n_keys], _NT, preferred_element_type=_F32)
        sel = None
        if i > topk:
            gi = gate[i * blk:(i + 1) * blk]
            rank = jnp.zeros((blk, LANES), jnp.int32)
            for jp in range(i):
                cj = gi[:, jp:jp + 1]
                beats = (cj > gi) | ((cj == gi) & (lane > jp))
                rank += beats.astype(jnp.int32)
            sel = rank < topk
        parts = []
        for j in range(i + 1):
            dist = rc + float((i - j) * blk)
            lj = s[:, j * blk:(j + 1) * blk] * scale - slope * dist
            if j == i:
                lj = jnp.where(rc >= 0, lj, neg_inf)
            elif sel is not None:
                lj = jnp.where(sel[:, j:j + 1], lj, neg_inf)
            parts.append(lj)
        m = parts[0].max(axis=-1, keepdims=True)
        for lj in parts[1:]:
            m = jnp.maximum(m, lj.max(axis=-1, keepdims=True))
        ps = [jnp.exp(lj - m) for lj in parts]
        den = ps[0].sum(axis=-1, keepdims=True)
        for pj in ps[1:]:
            den += pj.sum(axis=-1, keepdims=True)
        p = jnp.concatenate([pj.astype(_BF16) for pj in ps], axis=-1)
        pv = jnp.dot(p, vb[:n_keys], preferred_element_type=_F32)
        o_ref[0, i * blk:(i + 1) * blk, :] = (pv / den).astype(o_ref.dtype)


def _moba(proj, q_gain, k_gain, slopes, *, heads, q_col, k_col, v_col):
    b, s, _ = proj.shape
    d = HEAD_DIM
    blk_spec = lambda c0: pl.BlockSpec((1, s, d), lambda i, h: (i, 0, c0 // d + h))
    return pl.pallas_call(
        functools.partial(_moba_kernel, blk=MOBA_BLOCK, topk=MOBA_TOPK),
        grid=(b, heads),
        in_specs=[blk_spec(q_col), blk_spec(k_col), blk_spec(v_col),
                  pl.BlockSpec((1, d), lambda i, h: (0, 0)),
                  pl.BlockSpec((1, d), lambda i, h: (0, 0)),
                  pl.BlockSpec((1, 1, LANES), lambda i, h: (h, 0, 0))],
        out_specs=pl.BlockSpec((1, s, d), lambda i, h: (i, 0, h)),
        out_shape=jax.ShapeDtypeStruct((b, s, heads * d), _BF16),
        compiler_params=_params("parallel", "parallel"),
        name="moba_attn",
    )(proj, proj, proj, q_gain.reshape(1, d), k_gain.reshape(1, d), slopes)


def _swa_kernel(*refs, win, groups, lane_off, nq, nkv):
    q_refs, k_refs, v_refs = refs[:nq], refs[nq:nq + nkv], refs[nq + nkv:nq + 2 * nkv]
    qg_ref, kg_ref, slope_ref, sink_ref, o_ref, qbuf, kbuf, vbuf = refs[nq + 2 * nkv:]
    s_len, d = k_refs[0].shape[1], k_refs[0].shape[2]

    def cols(block_refs, width):
        cat = jnp.concatenate([r[0] for r in block_refs], axis=1)
        return cat[:, lane_off:lane_off + width]

    kbuf[0:win] = jnp.zeros((win, d), _BF16)
    vbuf[0:win] = jnp.zeros((win, d), _BF16)
    kbuf[win:] = _rms(cols(k_refs, d), kg_ref[...]).astype(_BF16)
    vbuf[win:] = cols(v_refs, d).astype(_BF16)
    q_all = cols(q_refs, groups * d)
    for g in range(groups):
        qbuf[g] = _rms(q_all[:, g * d:(g + 1) * d], qg_ref[...]).astype(_BF16)

    rows = groups * win
    ri = lax.broadcasted_iota(jnp.int32, (rows, 2 * win), 0) % win
    ci = lax.broadcasted_iota(jnp.int32, (rows, 2 * win), 1)
    dist_i = win + ri - ci
    dist = dist_i.astype(_F32)
    band = (dist_i >= 0) & (dist_i < win)
    cur = ci >= win
    slope_col = jnp.concatenate(
        [jnp.broadcast_to(slope_ref[0][g:g + 1, 0:1], (win, 1)) for g in range(groups)], axis=0)
    sink_col = jnp.concatenate(
        [jnp.broadcast_to(sink_ref[0][g:g + 1, 0:1], (win, 1)) for g in range(groups)], axis=0)
    alibi = slope_col * dist
    scale = d ** -0.5

    def body(n, carry):
        st = pl.multiple_of(n * win, win)
        qs = jnp.concatenate([qbuf[g, pl.ds(st, win), :] for g in range(groups)], axis=0)
        kw = kbuf[pl.ds(st, 2 * win), :]
        vw = vbuf[pl.ds(st, 2 * win), :]
        s = lax.dot_general(qs, kw, _NT, preferred_element_type=_F32)
        mask = band & (cur | (n > 0))
        l = jnp.where(mask, s * scale - alibi, -jnp.inf)
        m = jnp.maximum(l.max(axis=-1, keepdims=True), sink_col)
        p = jnp.exp(l - m)
        den = p.sum(axis=-1, keepdims=True) + jnp.exp(sink_col - m)
        o = jnp.dot(p.astype(_BF16), vw, preferred_element_type=_F32) / den
        for g in range(groups):
            o_ref[0, pl.ds(st, win), g * d:(g + 1) * d] = o[g * win:(g + 1) * win].astype(o_ref.dtype)
        return carry

    lax.fori_loop(0, s_len // win, body, 0, unroll=4)


def _swa(proj, q_gain, k_gain, slopes, sinks, *, kv_heads, groups, q_col, k_col, v_col):
    b, s, _ = proj.shape
    d = HEAD_DIM
    win = SWA_WINDOW
    lane_off = q_col % LANES
    assert k_col % LANES == lane_off and v_col % LANES == lane_off
    extra = 1 if lane_off else 0
    nq, nkv = groups + extra, 1 + extra

    def block_specs(c0, per_head, count):
        return [pl.BlockSpec((1, s, LANES),
                             lambda i, h, t=t: (i, 0, c0 // LANES + h * per_head + t))
                for t in range(count)]

    in_specs = (block_specs(q_col, groups, nq) + block_specs(k_col, 1, nkv)
                + block_specs(v_col, 1, nkv)
                + [pl.BlockSpec((1, d), lambda i, h: (0, 0)),
                   pl.BlockSpec((1, d), lambda i, h: (0, 0)),
                   pl.BlockSpec((1, groups, LANES), lambda i, h: (h, 0, 0)),
                   pl.BlockSpec((1, groups, LANES), lambda i, h: (h, 0, 0))])
    return pl.pallas_call(
        functools.partial(_swa_kernel, win=win, groups=groups, lane_off=lane_off, nq=nq, nkv=nkv),
        grid=(b, kv_heads),
        in_specs=in_specs,
        out_specs=pl.BlockSpec((1, s, groups * d), lambda i, h: (i, 0, h)),
        out_shape=jax.ShapeDtypeStruct((b, s, kv_heads * groups * d), _BF16),
        scratch_shapes=[pltpu.VMEM((groups, s, d), _BF16),
                        pltpu.VMEM((s + win, d), _BF16),
                        pltpu.VMEM((s + win, d), _BF16)],
        compiler_params=_params("parallel", "parallel"),
        name="swa_attn",
    )(*([proj] * (nq + 2 * nkv)), q_gain.reshape(1, d), k_gain.reshape(1, d), slopes, sinks)


def _gla_kernel(q_ref, k_ref, v_ref, g_ref, a_ref, wa_ref, ba_ref, gn_ref, o_ref,
                qe_s, ke_s, kd_s, dec_s, o_s, *, chunk, dk, dv, rank):
    s_len = q_ref.shape[1]
    n_chunks = s_len // chunk
    pair_k = q_ref.shape[2]
    pair_v = v_ref.shape[2]
    hi = lax.Precision.HIGHEST

    a = a_ref[0]
    a = jnp.where(lax.broadcasted_iota(jnp.int32, a.shape, 1) < rank, a, 0.0)
    z = jnp.dot(a, wa_ref[...], precision=hi, preferred_element_type=_F32) + ba_ref[...]
    log_a = (jnp.minimum(z, 0.0) - jnp.log1p(jnp.exp(-jnp.abs(z)))) * (1.0 / GLA_TAU)
    rin = lax.broadcasted_iota(jnp.int32, (s_len, pair_k), 0) % chunk
    lam = log_a
    shift = 1
    while shift < chunk:
        lam = lam + jnp.where(rin >= shift, pltpu.roll(lam, shift, 0), 0.0)
        shift *= 2
    lam3 = lam.reshape(n_chunks, chunk, pair_k)
    last3 = lam3[:, chunk - 1:chunk, :]
    lam_last = jnp.broadcast_to(last3, (n_chunks, chunk, pair_k)).reshape(s_len, pair_k)
    kk = k_ref[0]
    qe_s[...] = q_ref[0] * (dk ** -0.5) * jnp.exp(lam)
    ke_s[...] = kk * jnp.exp(-lam)
    kd_s[...] = kk * jnp.exp(lam_last - lam)
    dec_s[...] = jnp.exp(last3)

    lane_k = lax.broadcasted_iota(jnp.int32, (chunk, pair_k), 1)
    head0 = lane_k < dk
    tril = (lax.broadcasted_iota(jnp.int32, (chunk, chunk), 0)
            >= lax.broadcasted_iota(jnp.int32, (chunk, chunk), 1))
    srow = lax.broadcasted_iota(jnp.int32, (pair_v, pair_k), 0)
    scol = lax.broadcasted_iota(jnp.int32, (pair_v, pair_k), 1)
    same_head = (srow < dv) == (scol < dk)

    def body(n, state):
        r0 = pl.multiple_of(n * chunk, chunk)
        qe = qe_s[pl.ds(r0, chunk), :]
        ke = ke_s[pl.ds(r0, chunk), :]
        kd = kd_s[pl.ds(r0, chunk), :]
        vn = v_ref[0, pl.ds(r0, chunk), :]
        att0 = lax.dot_general(jnp.where(head0, qe, 0.0), ke, _NT, precision=hi,
                               preferred_element_type=_F32)
        att1 = lax.dot_general(jnp.where(head0, 0.0, qe), ke, _NT, precision=hi,
                               preferred_element_type=_F32)
        o0 = jnp.dot(jnp.where(tril, att0, 0.0), vn[:, :dv], precision=hi,
                     preferred_element_type=_F32)
        o1 = jnp.dot(jnp.where(tril, att1, 0.0), vn[:, dv:], precision=hi,
                     preferred_element_type=_F32)
        o_inter = lax.dot_general(qe, state, _NT, precision=hi, preferred_element_type=_F32)
        o_s[pl.ds(r0, chunk), :] = jnp.concatenate([o0, o1], axis=1) + o_inter
        upd = lax.dot_general(vn, kd, _TN, precision=hi, preferred_element_type=_F32)
        return state * dec_s[n] + jnp.where(same_head, upd, 0.0)

    lax.fori_loop(0, n_chunks, body, jnp.zeros((pair_v, pair_k), _F32), unroll=2)

    gate = g_ref[0]
    for h in range(pair_v // dv):
        o = _rms(o_s[:, h * dv:(h + 1) * dv], gn_ref[...])
        gh = gate[:, h * dv:(h + 1) * dv]
        o_ref[0, :, h * dv:(h + 1) * dv] = (o * (gh * jax.nn.sigmoid(gh))).astype(o_ref.dtype)


def _gla(proj, w_a_pad, b_a, out_gain, *, heads, q_col, k_col, v_col, g_col, a_col):
    b, s, _ = proj.shape
    pairs = heads // 2
    pk, pv = 2 * GLA_DK, 2 * GLA_DV
    n_chunks = s // GLA_CHUNK
    assert a_col % LANES == 0
    return pl.pallas_call(
        functools.partial(_gla_kernel, chunk=GLA_CHUNK, dk=GLA_DK, dv=GLA_DV, rank=GLA_LOWRANK),
        grid=(b, pairs),
        in_specs=[pl.BlockSpec((1, s, pk), lambda i, p: (i, 0, q_col // pk + p)),
                  pl.BlockSpec((1, s, pk), lambda i, p: (i, 0, k_col // pk + p)),
                  pl.BlockSpec((1, s, pv), lambda i, p: (i, 0, v_col // pv + p)),
                  pl.BlockSpec((1, s, pv), lambda i, p: (i, 0, g_col // pv + p)),
                  pl.BlockSpec((1, s, LANES), lambda i, p: (i, 0, a_col // LANES)),
                  pl.BlockSpec((LANES, pk), lambda i, p: (0, p)),
                  pl.BlockSpec((1, pk), lambda i, p: (0, p)),
                  pl.BlockSpec((1, GLA_DV), lambda i, p: (0, 0))],
        out_specs=pl.BlockSpec((1, s, pv), lambda i, p: (i, 0, p)),
        out_shape=jax.ShapeDtypeStruct((b, s, heads * GLA_DV), _BF16),
        scratch_shapes=[pltpu.VMEM((s, pk), _F32), pltpu.VMEM((s, pk), _F32),
                        pltpu.VMEM((s, pk), _F32), pltpu.VMEM((n_chunks, 1, pk), _F32),
                        pltpu.VMEM((s, pv), _F32)],
        compiler_params=_params("parallel", "parallel"),
        name="gla_mixer",
    )(proj, proj, proj, proj, proj, w_a_pad, b_a.reshape(1, -1), out_gain.reshape(1, -1))


def _alibi_slopes(n_heads):
    return jnp.exp2(-ALIBI_MAX_EXP * jnp.arange(1, n_heads + 1, dtype=_F32) / n_heads)


def _lane_rows(vec, lead):
    return jnp.broadcast_to(vec.astype(_F32)[:, None], (vec.shape[0], LANES)).reshape(lead, -1, LANES)


def kernel(x, c, w_ada, b_ada, norm_attn, w_in, moba_q_norm, moba_k_norm, gla_w_a, gla_b_a,
           gla_out_norm, swa_q_norm, swa_k_norm, swa_sinks, w_out, norm_mlp, w_mlp_in, w_mlp_out):
    b, s, d = x.shape
    depth = w_ada.shape[0]
    d_ff = w_mlp_in.shape[-1]
    swa_heads = swa_sinks.shape[-1]
    gla_heads = gla_w_a.shape[-1] // GLA_DK
    moba_heads = (d - gla_heads * GLA_DV - swa_heads * HEAD_DIM) // HEAD_DIM
    groups = swa_heads // SWA_KV_HEADS

    n_moba = moba_heads * HEAD_DIM
    mq_c, mk_c, mv_c = 0, n_moba, 2 * n_moba
    gq_c = 3 * n_moba
    gk_c = gq_c + gla_heads * GLA_DK
    gv_c = gk_c + gla_heads * GLA_DK
    gg_c = gv_c + gla_heads * GLA_DV
    ga_c = gg_c + gla_heads * GLA_DV
    sq_c = ga_c + GLA_LOWRANK
    sk_c = sq_c + swa_heads * HEAD_DIM
    sv_c = sk_c + SWA_KV_HEADS * HEAD_DIM

    moba_slopes = _lane_rows(_alibi_slopes(moba_heads), moba_heads)
    swa_slopes = _lane_rows(_alibi_slopes(swa_heads), SWA_KV_HEADS)

    c_pad = jnp.pad(c, ((0, 8 - b), (0, 0)))
    mod_all = _ada(c_pad, w_ada, b_ada)

    for l in range(depth):
        mod = mod_all[l, :b].reshape(b * 6, 1, d)
        w_a_pad = jnp.pad(gla_w_a[l], ((0, LANES - GLA_LOWRANK), (0, 0)))

        h = _norm_mod(x, norm_attn[l], mod, 1, 0).reshape(b * s, d)
        proj = _matmul(h, w_in, l, out_dtype=_F32, name="in_proj").reshape(b, s, -1)

        y_moba = _moba(proj, moba_q_norm[l], moba_k_norm[l], moba_slopes,
                       heads=moba_heads, q_col=mq_c, k_col=mk_c, v_col=mv_c)
        y_gla = _gla(proj, w_a_pad, gla_b_a[l], gla_out_norm[l], heads=gla_heads,
                     q_col=gq_c, k_col=gk_c, v_col=gv_c, g_col=gg_c, a_col=ga_c)
        y_swa = _swa(proj, swa_q_norm[l], swa_k_norm[l], swa_slopes,
                     _lane_rows(swa_sinks[l], SWA_KV_HEADS),
                     kv_heads=SWA_KV_HEADS, groups=groups, q_col=sq_c, k_col=sk_c, v_col=sv_c)
        x = _outproj(y_moba, y_gla, y_swa, w_out, l, x, mod, 2)

        h = _norm_mod(x, norm_mlp[l], mod, 4, 3).reshape(b * s, d)
        hid = _matmul(h, w_mlp_in, l, out_dtype=_BF16, relu2=True, name="mlp_in")
        x = _mlp_out(hid.reshape(b, s, d_ff), w_mlp_out, l, x, mod, 5)
    return x
```

```python
import functools

import jax
import jax.numpy as jnp
from jax import lax
from jax.experimental import pallas as pl
from jax.experimental.pallas import tpu as pltpu

HEAD_DIM = 128
MOBA_BLOCK = 256
MOBA_TOPK = 3
GLA_DV = 128
GLA_DK = 64
GLA_LOWRANK = 16
GLA_TAU = 16.0
GLA_CHUNK = 64
SWA_KV_HEADS = 4
SWA_WINDOW = 128
ALIBI_MAX_EXP = 8.0
EPS = 1e-6

LANES = 128
VMEM_LIMIT_BYTES = 56 * 1024 * 1024

_F32 = jnp.float32
_BF16 = jnp.bfloat16
_NT = (((1,), (1,)), ((), ()))
_TN = (((0,), (0,)), ((), ()))
_LOG2E = 1.4426950408889634


def _params(*sem):
    return pltpu.CompilerParams(dimension_semantics=sem, vmem_limit_bytes=VMEM_LIMIT_BYTES)


def _rms(x, gain):
    return x * lax.rsqrt(jnp.mean(x * x, axis=-1, keepdims=True) + EPS) * gain


def _ada_kernel(c_ref, w_ref, b_ref, o_ref):
    kk = pl.program_id(1)
    c = c_ref[...]
    cond = (c * jax.nn.sigmoid(c)).astype(_BF16)
    part = jnp.dot(cond, w_ref[0].astype(_BF16), preferred_element_type=_F32)

    @pl.when(kk == 0)
    def _():
        o_ref[0] = part + b_ref[0]

    @pl.when(kk > 0)
    def _():
        o_ref[0] += part


def _ada(c_pad, w_ada, b_ada, *, tk=128):
    depth, d, n = w_ada.shape
    rows = c_pad.shape[0]
    return pl.pallas_call(
        _ada_kernel,
        grid=(depth, d // tk),
        in_specs=[pl.BlockSpec((rows, tk), lambda l, k: (0, k)),
                  pl.BlockSpec((1, tk, n), lambda l, k: (l, k, 0)),
                  pl.BlockSpec((1, 1, n), lambda l, k: (l, 0, 0))],
        out_specs=pl.BlockSpec((1, rows, n), lambda l, k: (l, 0, 0)),
        out_shape=jax.ShapeDtypeStruct((depth, rows, n), _F32),
        compiler_params=_params("parallel", "arbitrary"),
        name="ada_mod",
    )(c_pad, w_ada, b_ada.reshape(depth, 1, n))


def _norm_mod_kernel(x_ref, gain_ref, sc_ref, sh_ref, o_ref):
    x = x_ref[0]
    inv = lax.rsqrt(jnp.mean(x * x, axis=-1, keepdims=True) + EPS)
    y = x * inv * gain_ref[...]
    o_ref[0] = (y * (1.0 + sc_ref[0]) + sh_ref[0]).astype(o_ref.dtype)


def _norm_mod(x, gain, mod, sc_idx, sh_idx, *, ts=256):
    b, s, d = x.shape
    return pl.pallas_call(
        _norm_mod_kernel,
        grid=(b, s // ts),
        in_specs=[pl.BlockSpec((1, ts, d), lambda i, t: (i, t, 0)),
                  pl.BlockSpec((1, d), lambda i, t: (0, 0)),
                  pl.BlockSpec((1, 1, d), lambda i, t: (i * 6 + sc_idx, 0, 0)),
                  pl.BlockSpec((1, 1, d), lambda i, t: (i * 6 + sh_idx, 0, 0))],
        out_specs=pl.BlockSpec((1, ts, d), lambda i, t: (i, t, 0)),
        out_shape=jax.ShapeDtypeStruct((b, s, d), _BF16),
        compiler_params=_params("parallel", "parallel"),
        name="norm_mod",
    )(x, gain.reshape(1, d), mod, mod)


def _mm_kernel(x_ref, w_ref, o_ref, *, relu2):
    acc = jnp.dot(x_ref[...], w_ref[0].astype(_BF16), preferred_element_type=_F32)
    if relu2:
        acc = jnp.square(jnp.maximum(acc, 0.0))
    o_ref[...] = acc.astype(o_ref.dtype)


def _matmul(x, w_all, layer, *, tm=1024, tn=512, out_dtype, relu2=False, name):
    m, k = x.shape
    n = w_all.shape[2]
    return pl.pallas_call(
        functools.partial(_mm_kernel, relu2=relu2),
        grid=(m // tm, pl.cdiv(n, tn)),
        in_specs=[pl.BlockSpec((tm, k), lambda i, j: (i, 0)),
                  pl.BlockSpec((1, k, tn), lambda i, j: (layer, 0, j))],
        out_specs=pl.BlockSpec((tm, tn), lambda i, j: (i, j)),
        out_shape=jax.ShapeDtypeStruct((m, n), out_dtype),
        compiler_params=_params("parallel", "parallel"),
        name=name,
    )(x, w_all)


def _mm_nt_kernel(x_ref, wt_ref, o_ref):
    o_ref[...] = lax.dot_general(x_ref[...], wt_ref[0].astype(_BF16), _NT,
                                 preferred_element_type=_F32)


def _matmul_nt(x, wt_all, layer, n_out, *, tm=1024, tn=512, name):
    m, k = x.shape
    tn = min(tn, n_out)
    return pl.pallas_call(
        _mm_nt_kernel,
        grid=(m // tm, n_out // tn),
        in_specs=[pl.BlockSpec((tm, k), lambda i, j: (i, 0)),
                  pl.BlockSpec((1, tn, k), lambda i, j: (layer, j, 0))],
        out_specs=pl.BlockSpec((tm, tn), lambda i, j: (i, j)),
        out_shape=jax.ShapeDtypeStruct((m, n_out), _F32),
        compiler_params=_params("parallel", "parallel"),
        name=name,
    )(x, wt_all)


def _outproj_kernel(ym_ref, yg_ref, ys_ref, w_ref, x_ref, g_ref, o_ref):
    km = ym_ref.shape[-1]
    kg = yg_ref.shape[-1]
    w = w_ref[0].astype(_BF16)
    acc = jnp.dot(ym_ref[0], w[0:km, :], preferred_element_type=_F32)
    acc += jnp.dot(yg_ref[0], w[km:km + kg, :], preferred_element_type=_F32)
    acc += jnp.dot(ys_ref[0], w[km + kg:, :], preferred_element_type=_F32)
    o_ref[0] = x_ref[0] + g_ref[0] * acc


def _outproj(y_moba, y_gla, y_swa, w_all, layer, x, mod, gate_idx, *, tm=1024, tn=512):
    b, s, d = x.shape
    k = w_all.shape[1]
    km, kg, ks = y_moba.shape[-1], y_gla.shape[-1], y_swa.shape[-1]
    return pl.pallas_call(
        _outproj_kernel,
        grid=(b, s // tm, d // tn),
        in_specs=[pl.BlockSpec((1, tm, km), lambda i, t, j: (i, t, 0)),
                  pl.BlockSpec((1, tm, kg), lambda i, t, j: (i, t, 0)),
                  pl.BlockSpec((1, tm, ks), lambda i, t, j: (i, t, 0)),
                  pl.BlockSpec((1, k, tn), lambda i, t, j: (layer, 0, j)),
                  pl.BlockSpec((1, tm, tn), lambda i, t, j: (i, t, j)),
                  pl.BlockSpec((1, 1, tn), lambda i, t, j: (i * 6 + gate_idx, 0, j))],
        out_specs=pl.BlockSpec((1, tm, tn), lambda i, t, j: (i, t, j)),
        out_shape=jax.ShapeDtypeStruct((b, s, d), _F32),
        compiler_params=_params("parallel", "parallel", "parallel"),
        name="out_proj",
    )(y_moba, y_gla, y_swa, w_all, x, mod)


def _mlp_out_kernel(h_ref, w_ref, x_ref, g_ref, o_ref, acc_ref, *, nk):
    kk = pl.program_id(3)
    j = pl.program_id(4)
    part = jnp.dot(h_ref[0], w_ref[0].astype(_BF16), preferred_element_type=_F32)

    @pl.when(kk == 0)
    def _():
        acc_ref[j] = part

    @pl.when((kk > 0) & (kk < nk - 1))
    def _():
        acc_ref[j] += part

    @pl.when(kk == nk - 1)
    def _():
        o_ref[0] = x_ref[0] + g_ref[0] * (acc_ref[j] + part)


def _mlp_out(h, w_all, layer, x, mod, gate_idx, *, tm=1024, tn=512, tk=4096, n_groups=2):
    b, s, d = x.shape
    f = w_all.shape[1]
    nk = f // tk
    nj = d // (n_groups * tn)
    assert nk >= 2

    def out_idx(i, t, gidx, k, j):
        return (i, t, gidx * nj + jnp.where(k == nk - 1, j, 0))

    return pl.pallas_call(
        functools.partial(_mlp_out_kernel, nk=nk),
        grid=(b, s // tm, n_groups, nk, nj),
        in_specs=[pl.BlockSpec((1, tm, tk), lambda i, t, gidx, k, j: (i, t, k)),
                  pl.BlockSpec((1, tk, tn), lambda i, t, gidx, k, j: (layer, k, gidx * nj + j)),
                  pl.BlockSpec((1, tm, tn), out_idx),
                  pl.BlockSpec((1, 1, tn),
                               lambda i, t, gidx, k, j: (i * 6 + gate_idx, 0, out_idx(i, t, gidx, k, j)[2]))],
        out_specs=pl.BlockSpec((1, tm, tn), out_idx),
        out_shape=jax.ShapeDtypeStruct((b, s, d), _F32),
        scratch_shapes=[pltpu.VMEM((nj, tm, tn), _F32)],
        compiler_params=_params("parallel", "parallel", "parallel", "arbitrary", "arbitrary"),
        name="mlp_out",
    )(h, w_all, x, mod)


def _moba_kernel(q_ref, k_ref, v_ref, qg_ref, kg_ref, slope_ref, o_ref,
                 vaug, bias_s, l_s, p_s, *, blk, topk):
    s_len, d = q_ref.shape[1], q_ref.shape[2]
    nb = s_len // blk
    q = _rms(q_ref[0], qg_ref[...])
    k = _rms(k_ref[0], kg_ref[...])
    kmean = jnp.sum(k.reshape(nb, blk, d), axis=1) * (1.0 / blk)
    gate_t = lax.dot_general(kmean, q, _NT, precision=lax.Precision.HIGHEST,
                             preferred_element_type=_F32)
    qb = q.astype(_BF16)
    kb = k.astype(_BF16)
    vaug[:, :d] = v_ref[0].astype(_BF16)
    vaug[:, d:] = jnp.ones((s_len, d), _BF16)

    slope2 = slope_ref[0][:, 0:1] * _LOG2E
    c1 = (d ** -0.5) * _LOG2E
    key_i = lax.broadcasted_iota(jnp.int32, (blk, blk), 0)
    qry_i = lax.broadcasted_iota(jnp.int32, (blk, blk), 1)
    rel = (qry_i - key_i).astype(_F32)
    bias_s[0] = jnp.where(qry_i >= key_i, slope2 * rel, jnp.inf)
    for dd in range(1, nb):
        bias_s[dd] = slope2 * (rel + float(dd * blk))
    cand = lax.broadcasted_iota(jnp.int32, (nb, blk), 0)

    for i in range(nb):
        n_keys = (i + 1) * blk
        qi = qb[i * blk:(i + 1) * blk]
        s_t = lax.dot_general(kb[:n_keys], qi, _NT, preferred_element_type=_F32)
        sel_bias = None
        if i > topk:
            g = gate_t[:, i * blk:(i + 1) * blk]
            rank = jnp.zeros((nb, blk), jnp.int32)
            for jp in range(i):
                gj = g[jp:jp + 1, :]
                beats = (gj > g) | ((gj == g) & (cand > jp))
                rank += beats.astype(jnp.int32)
            sel_bias = jnp.where(rank < topk, 0.0, -jnp.inf)
        m8 = jnp.full((8, blk), -jnp.inf, _F32)
        for j in range(i + 1):
            t = s_t[j * blk:(j + 1) * blk] * c1 - bias_s[i - j]
            if sel_bias is not None and j < i:
                t = t + sel_bias[j:j + 1, :]
            l_s[j * blk:(j + 1) * blk, :] = t
            m8 = jnp.maximum(m8, jnp.max(t.reshape(blk // 8, 8, blk), axis=0))
        m = jnp.max(m8, axis=0, keepdims=True)
        for j in range(i + 1):
            p_s[j * blk:(j + 1) * blk, :] = jnp.exp2(l_s[j * blk:(j + 1) * blk, :] - m).astype(_BF16)
        o2 = lax.dot_general(p_s[0:n_keys, :], vaug[0:n_keys, :], _TN,
                             preferred_element_type=_F32)
        o_ref[0, i * blk:(i + 1) * blk, :] = (o2[:, :d] / o2[:, d:]).astype(o_ref.dtype)


def _moba(proj, q_gain, k_gain, slopes, *, heads, q_col, k_col, v_col):
    b, s, _ = proj.shape
    d = HEAD_DIM
    blk = MOBA_BLOCK
    blk_spec = lambda c0: pl.BlockSpec((1, s, d), lambda i, h: (i, 0, c0 // d + h))
    return pl.pallas_call(
        functools.partial(_moba_kernel, blk=blk, topk=MOBA_TOPK),
        grid=(b, heads),
        in_specs=[blk_spec(q_col), blk_spec(k_col), blk_spec(v_col),
                  pl.BlockSpec((1, d), lambda i, h: (0, 0)),
                  pl.BlockSpec((1, d), lambda i, h: (0, 0)),
                  pl.BlockSpec((1, 1, LANES), lambda i, h: (h, 0, 0))],
        out_specs=pl.BlockSpec((1, s, d), lambda i, h: (i, 0, h)),
        out_shape=jax.ShapeDtypeStruct((b, s, heads * d), _BF16),
        scratch_shapes=[pltpu.VMEM((s, 2 * d), _BF16),
                        pltpu.VMEM((s // blk, blk, blk), _F32),
                        pltpu.VMEM((s, blk), _F32),
                        pltpu.VMEM((s, blk), _BF16)],
        compiler_params=_params("parallel", "parallel"),
        name="moba_attn",
    )(proj, proj, proj, q_gain.reshape(1, d), k_gain.reshape(1, d), slopes)


def _swa_kernel(q_ref, k_ref, v_ref, qg_ref, kg_ref, slope_ref, sink_ref, o_ref,
                qbuf, kbuf, vbuf, bias_s, *, win, groups):
    s_len, d = k_ref.shape[1], k_ref.shape[2]
    nq_rows = groups * win
    kbuf[0:win] = jnp.zeros((win, d), _BF16)
    kbuf[win:] = _rms(k_ref[0], kg_ref[...]).astype(_BF16)
    vbuf[0:win] = jnp.zeros((win, d), _BF16)
    vbuf[win:] = v_ref[0].astype(_BF16)
    for g in range(groups):
        qbuf[g] = _rms(q_ref[0, :, g * d:(g + 1) * d], qg_ref[...]).astype(_BF16)

    key_i = lax.broadcasted_iota(jnp.int32, (2 * win, nq_rows), 0)
    qry_i = lax.broadcasted_iota(jnp.int32, (2 * win, nq_rows), 1) % win
    dist_i = win + qry_i - key_i
    band = (dist_i >= 0) & (dist_i < win)
    slope_row = jnp.concatenate(
        [jnp.broadcast_to(slope_ref[0][g:g + 1, :], (1, win)) for g in range(groups)], axis=1)
    sink2 = jnp.concatenate(
        [jnp.broadcast_to(sink_ref[0][g:g + 1, :], (1, win)) for g in range(groups)], axis=1) * _LOG2E
    alibi2 = (slope_row * _LOG2E) * dist_i.astype(_F32)
    bias_s[0] = jnp.where(band & (key_i >= win), alibi2, jnp.inf)
    bias_s[1] = jnp.where(band, alibi2, jnp.inf)
    c1 = (d ** -0.5) * _LOG2E

    def body(n, carry):
        st = pl.multiple_of(n * win, win)
        qs = jnp.concatenate([qbuf[g, pl.ds(st, win), :] for g in range(groups)], axis=0)
        kw = kbuf[pl.ds(st, 2 * win), :]
        vw = vbuf[pl.ds(st, 2 * win), :]
        s_t = lax.dot_general(kw, qs, _NT, preferred_element_type=_F32)
        l2 = s_t * c1 - bias_s[jnp.minimum(n, 1)]
        m = jnp.maximum(jnp.max(l2, axis=0, keepdims=True), sink2)
        p = jnp.exp2(l2 - m)
        den = jnp.sum(p, axis=0, keepdims=True) + jnp.exp2(sink2 - m)
        pn = (p * (1.0 / den)).astype(_BF16)
        o = lax.dot_general(pn, vw, _TN, preferred_element_type=_F32)
        for g in range(groups):
            o_ref[0, pl.ds(st, win), g * d:(g + 1) * d] = o[g * win:(g + 1) * win].astype(o_ref.dtype)
        return carry

    lax.fori_loop(0, s_len // win, body, 0, unroll=4)


def _swa(proj, q_gain, k_gain, slopes, sinks, *, kv_heads, groups, q_col, k_col, v_col):
    b, s, _ = proj.shape
    d = HEAD_DIM
    win = SWA_WINDOW
    return pl.pallas_call(
        functools.partial(_swa_kernel, win=win, groups=groups),
        grid=(b, kv_heads),
        in_specs=[pl.BlockSpec((1, s, groups * d), lambda i, h: (i, 0, q_col // (groups * d) + h)),
                  pl.BlockSpec((1, s, d), lambda i, h: (i, 0, k_col // d + h)),
                  pl.BlockSpec((1, s, d), lambda i, h: (i, 0, v_col // d + h)),
                  pl.BlockSpec((1, d), lambda i, h: (0, 0)),
                  pl.BlockSpec((1, d), lambda i, h: (0, 0)),
                  pl.BlockSpec((1, groups, LANES), lambda i, h: (h, 0, 0)),
                  pl.BlockSpec((1, groups, LANES), lambda i, h: (h, 0, 0))],
        out_specs=pl.BlockSpec((1, s, groups * d), lambda i, h: (i, 0, h)),
        out_shape=jax.ShapeDtypeStruct((b, s, kv_heads * groups * d), _BF16),
        scratch_shapes=[pltpu.VMEM((groups, s, d), _BF16),
                        pltpu.VMEM((s + win, d), _BF16),
                        pltpu.VMEM((s + win, d), _BF16),
                        pltpu.VMEM((2, 2 * win, groups * win), _F32)],
        compiler_params=_params("parallel", "parallel"),
        name="swa_attn",
    )(proj, proj, proj, q_gain.reshape(1, d), k_gain.reshape(1, d), slopes, sinks)


def _gla_kernel(q_ref, k_ref, v_ref, g_ref, a_ref, wa_ref, ba_ref, gn_ref, o_ref,
                qe_s, ke_s, kd_s, v_s, dec_s, o_s, st_s, *, chunk, dk, dv):
    s_len = q_ref.shape[1]
    n_chunks = s_len // chunk
    pair_k = q_ref.shape[2]
    pair_v = v_ref.shape[2]

    z = jnp.dot(a_ref[0], wa_ref[...], precision=lax.Precision.HIGHEST,
                preferred_element_type=_F32) + ba_ref[...]
    log_a = (jnp.minimum(z, 0.0) - jnp.log1p(jnp.exp(-jnp.abs(z)))) * (1.0 / GLA_TAU)
    rin = lax.broadcasted_iota(jnp.int32, (s_len, pair_k), 0) % chunk
    lam = log_a
    shift = 1
    while shift < chunk:
        lam = lam + jnp.where(rin >= shift, pltpu.roll(lam, shift, 0), 0.0)
        shift *= 2
    last3 = lam.reshape(n_chunks, chunk, pair_k)[:, chunk - 1:chunk, :]
    dec3 = jnp.exp(last3)
    dec_rows = jnp.broadcast_to(dec3, (n_chunks, chunk, pair_k)).reshape(s_len, pair_k)
    ke = k_ref[0] * jnp.exp(-lam)
    qe_s[...] = (q_ref[0] * (dk ** -0.5) * jnp.exp(lam)).astype(_BF16)
    ke_s[...] = ke.astype(_BF16)
    kd_s[...] = (ke * dec_rows).astype(_BF16)
    dec_s[...] = dec3
    v_s[...] = v_ref[0].astype(_BF16)

    head0 = lax.broadcasted_iota(jnp.int32, (chunk, pair_k), 1) < dk
    tril = (lax.broadcasted_iota(jnp.int32, (chunk, chunk), 0)
            >= lax.broadcasted_iota(jnp.int32, (chunk, chunk), 1))
    srow = lax.broadcasted_iota(jnp.int32, (pair_v, pair_k), 0)
    scol = lax.broadcasted_iota(jnp.int32, (pair_v, pair_k), 1)
    same_head = (srow < dv) == (scol < dk)
    zero_b = jnp.zeros((), _BF16)

    def intra(n, carry):
        r0 = pl.multiple_of(n * chunk, chunk)
        qe = qe_s[pl.ds(r0, chunk), :]
        ke_n = ke_s[pl.ds(r0, chunk), :]
        vn = v_s[pl.ds(r0, chunk), :]
        att0 = lax.dot_general(jnp.where(head0, qe, zero_b), ke_n, _NT, preferred_element_type=_F32)
        att1 = lax.dot_general(jnp.where(head0, zero_b, qe), ke_n, _NT, preferred_element_type=_F32)
        o0 = jnp.dot(jnp.where(tril, att0, 0.0).astype(_BF16), vn[:, :dv], preferred_element_type=_F32)
        o1 = jnp.dot(jnp.where(tril, att1, 0.0).astype(_BF16), vn[:, dv:], preferred_element_type=_F32)
        o_s[pl.ds(r0, chunk), :] = jnp.concatenate([o0, o1], axis=1)
        upd = lax.dot_general(vn, kd_s[pl.ds(r0, chunk), :], _TN, preferred_element_type=_F32)
        st_s[n] = jnp.where(same_head, upd, 0.0)
        return carry

    lax.fori_loop(0, n_chunks, intra, 0, unroll=4)

    def scan(n, state):
        upd = st_s[n]
        st_s[n] = state
        return state * dec_s[n] + upd

    lax.fori_loop(0, n_chunks, scan, jnp.zeros((pair_v, pair_k), _F32), unroll=4)

    def inter(n, carry):
        r0 = pl.multiple_of(n * chunk, chunk)
        o_s[pl.ds(r0, chunk), :] += lax.dot_general(
            qe_s[pl.ds(r0, chunk), :], st_s[n].astype(_BF16), _NT, preferred_element_type=_F32)
        return carry

    lax.fori_loop(0, n_chunks, inter, 0, unroll=4)

    gate = g_ref[0]
    for h in range(pair_v // dv):
        o = _rms(o_s[:, h * dv:(h + 1) * dv], gn_ref[...])
        gh = gate[:, h * dv:(h + 1) * dv]
        o_ref[0, :, h * dv:(h + 1) * dv] = (o * (gh * jax.nn.sigmoid(gh))).astype(o_ref.dtype)


def _gla(proj, low, w_a_pad, b_a, out_gain, *, heads, q_col, k_col, v_col, g_col):
    b, s, _ = proj.shape
    pairs = heads // 2
    pk, pv = 2 * GLA_DK, 2 * GLA_DV
    n_chunks = s // GLA_CHUNK
    return pl.pallas_call(
        functools.partial(_gla_kernel, chunk=GLA_CHUNK, dk=GLA_DK, dv=GLA_DV),
        grid=(b, pairs),
        in_specs=[pl.BlockSpec((1, s, pk), lambda i, p: (i, 0, q_col // pk + p)),
                  pl.BlockSpec((1, s, pk), lambda i, p: (i, 0, k_col // pk + p)),
                  pl.BlockSpec((1, s, pv), lambda i, p: (i, 0, v_col // pv + p)),
                  pl.BlockSpec((1, s, pv), lambda i, p: (i, 0, g_col // pv + p)),
                  pl.BlockSpec((1, s, LANES), lambda i, p: (i, 0, 0)),
                  pl.BlockSpec((LANES, pk), lambda i, p: (0, p)),
                  pl.BlockSpec((1, pk), lambda i, p: (0, p)),
                  pl.BlockSpec((1, GLA_DV), lambda i, p: (0, 0))],
        out_specs=pl.BlockSpec((1, s, pv), lambda i, p: (i, 0, p)),
        out_shape=jax.ShapeDtypeStruct((b, s, heads * GLA_DV), _BF16),
        scratch_shapes=[pltpu.VMEM((s, pk), _BF16), pltpu.VMEM((s, pk), _BF16),
                        pltpu.VMEM((s, pk), _BF16), pltpu.VMEM((s, pv), _BF16),
                        pltpu.VMEM((n_chunks, 1, pk), _F32), pltpu.VMEM((s, pv), _F32),
                        pltpu.VMEM((n_chunks, pv, pk), _F32)],
        compiler_params=_params("parallel", "parallel"),
        name="gla_mixer",
    )(proj, proj, proj, proj, low, w_a_pad, b_a.reshape(1, -1), out_gain.reshape(1, -1))


def _alibi_slopes(n_heads):
    return jnp.exp2(-ALIBI_MAX_EXP * jnp.arange(1, n_heads + 1, dtype=_F32) / n_heads)


def _lane_rows(vec, lead):
    return jnp.broadcast_to(vec.astype(_F32)[:, None], (vec.shape[0], LANES)).reshape(lead, -1, LANES)


def kernel(x, c, w_ada, b_ada, norm_attn, w_in, moba_q_norm, moba_k_norm, gla_w_a, gla_b_a,
           gla_out_norm, swa_q_norm, swa_k_norm, swa_sinks, w_out, norm_mlp, w_mlp_in, w_mlp_out):
    b, s, d = x.shape
    depth = w_ada.shape[0]
    d_ff = w_mlp_in.shape[-1]
    swa_heads = swa_sinks.shape[-1]
    gla_heads = gla_w_a.shape[-1] // GLA_DK
    moba_heads = (d - gla_heads * GLA_DV - swa_heads * HEAD_DIM) // HEAD_DIM
    groups = swa_heads // SWA_KV_HEADS

    n_moba = moba_heads * HEAD_DIM
    mq_c, mk_c, mv_c = 0, n_moba, 2 * n_moba
    gq_c = 3 * n_moba
    gk_c = gq_c + gla_heads * GLA_DK
    gv_c = gk_c + gla_heads * GLA_DK
    gg_c = gv_c + gla_heads * GLA_DV
    ga_c = gg_c + gla_heads * GLA_DV
    sq_c = ga_c + GLA_LOWRANK
    n_swa = (swa_heads + 2 * SWA_KV_HEADS) * HEAD_DIM
    sk_c = swa_heads * HEAD_DIM
    sv_c = sk_c + SWA_KV_HEADS * HEAD_DIM

    w_in_t = jnp.swapaxes(w_in, 1, 2)

    moba_slopes = _lane_rows(_alibi_slopes(moba_heads), moba_heads)
    swa_slopes = _lane_rows(_alibi_slopes(swa_heads), SWA_KV_HEADS)

    c_pad = jnp.pad(c, ((0, 8 - b), (0, 0)))
    mod_all = _ada(c_pad, w_ada, b_ada)

    for l in range(depth):
        mod = mod_all[l, :b].reshape(b * 6, 1, d)
        w_a_pad = jnp.pad(gla_w_a[l], ((0, LANES - GLA_LOWRANK), (0, 0)))
        w_swa_t = w_in_t[l, sq_c:sq_c + n_swa][None]
        w_low_t = jnp.pad(w_in_t[l, ga_c:sq_c], ((0, LANES - GLA_LOWRANK), (0, 0)))[None]

        h = _norm_mod(x, norm_attn[l], mod, 1, 0).reshape(b * s, d)
        proj = _matmul_nt(h, w_in_t, l, ga_c, name="in_proj").reshape(b, s, -1)
        proj_swa = _matmul_nt(h, w_swa_t, 0, n_swa, name="in_proj_swa").reshape(b, s, -1)
        low = _matmul_nt(h, w_low_t, 0, LANES, name="in_proj_low").reshape(b, s, -1)

        y_moba = _moba(proj, moba_q_norm[l], moba_k_norm[l], moba_slopes,
                       heads=moba_heads, q_col=mq_c, k_col=mk_c, v_col=mv_c)
        y_gla = _gla(proj, low, w_a_pad, gla_b_a[l], gla_out_norm[l], heads=gla_heads,
                     q_col=gq_c, k_col=gk_c, v_col=gv_c, g_col=gg_c)
        y_swa = _swa(proj_swa, swa_q_norm[l], swa_k_norm[l], swa_slopes,
                     _lane_rows(swa_sinks[l], SWA_KV_HEADS),
                     kv_heads=SWA_KV_HEADS, groups=groups, q_col=0, k_col=sk_c, v_col=sv_c)
        x = _outproj(y_moba, y_gla, y_swa, w_out, l, x, mod, 2)

        h = _norm_mod(x, norm_mlp[l], mod, 4, 3).reshape(b * s, d)
        hid = _matmul(h, w_mlp_in, l, out_dtype=_BF16, relu2=True, name="mlp_in")
        x = _mlp_out(hid.reshape(b, s, d_ff), w_mlp_out, l, x, mod, 5)
    return x
```

```python
import functools

import jax
import jax.numpy as jnp
from jax import lax
from jax.experimental import pallas as pl
from jax.experimental.pallas import tpu as pltpu

HEAD_DIM = 128
MOBA_BLOCK = 256
MOBA_TOPK = 3
GLA_DV = 128
GLA_DK = 64
GLA_LOWRANK = 16
GLA_TAU = 16.0
GLA_CHUNK = 64
GLA_GROUP = 4
SWA_KV_HEADS = 4
SWA_WINDOW = 128
ALIBI_MAX_EXP = 8.0
EPS = 1e-6

LANES = 128
VMEM_LIMIT_BYTES = 56 * 1024 * 1024

_F32 = jnp.float32
_BF16 = jnp.bfloat16
_NT = (((1,), (1,)), ((), ()))
_TN = (((0,), (0,)), ((), ()))
_LOG2E = 1.4426950408889634


def _params(*sem):
    return pltpu.CompilerParams(dimension_semantics=sem, vmem_limit_bytes=VMEM_LIMIT_BYTES)


def _rms(x, gain):
    return x * lax.rsqrt(jnp.mean(x * x, axis=-1, keepdims=True) + EPS) * gain


def _ada_kernel(c_ref, w_ref, b_ref, o_ref):
    kk = pl.program_id(1)
    c = c_ref[...]
    cond = (c * jax.nn.sigmoid(c)).astype(_BF16)
    part = jnp.dot(cond, w_ref[0].astype(_BF16), preferred_element_type=_F32)

    @pl.when(kk == 0)
    def _():
        o_ref[0] = part + b_ref[0]

    @pl.when(kk > 0)
    def _():
        o_ref[0] += part


def _ada(c_pad, w_ada, b_ada, *, tk=128):
    depth, d, n = w_ada.shape
    rows = c_pad.shape[0]
    return pl.pallas_call(
        _ada_kernel,
        grid=(depth, d // tk),
        in_specs=[pl.BlockSpec((rows, tk), lambda l, k: (0, k)),
                  pl.BlockSpec((1, tk, n), lambda l, k: (l, k, 0)),
                  pl.BlockSpec((1, 1, n), lambda l, k: (l, 0, 0))],
        out_specs=pl.BlockSpec((1, rows, n), lambda l, k: (l, 0, 0)),
        out_shape=jax.ShapeDtypeStruct((depth, rows, n), _F32),
        compiler_params=_params("parallel", "arbitrary"),
        name="ada_mod",
    )(c_pad, w_ada, b_ada.reshape(depth, 1, n))


def _norm_mod_kernel(x_ref, gain_ref, sc_ref, sh_ref, o_ref):
    x = x_ref[0]
    inv = lax.rsqrt(jnp.mean(x * x, axis=-1, keepdims=True) + EPS)
    y = x * inv * gain_ref[...]
    o_ref[0] = (y * (1.0 + sc_ref[0]) + sh_ref[0]).astype(o_ref.dtype)


def _norm_mod(x, gain, mod, sc_idx, sh_idx, *, ts=256):
    b, s, d = x.shape
    return pl.pallas_call(
        _norm_mod_kernel,
        grid=(b, s // ts),
        in_specs=[pl.BlockSpec((1, ts, d), lambda i, t: (i, t, 0)),
                  pl.BlockSpec((1, d), lambda i, t: (0, 0)),
                  pl.BlockSpec((1, 1, d), lambda i, t: (i * 6 + sc_idx, 0, 0)),
                  pl.BlockSpec((1, 1, d), lambda i, t: (i * 6 + sh_idx, 0, 0))],
        out_specs=pl.BlockSpec((1, ts, d), lambda i, t: (i, t, 0)),
        out_shape=jax.ShapeDtypeStruct((b, s, d), _BF16),
        compiler_params=_params("parallel", "parallel"),
        name="norm_mod",
    )(x, gain.reshape(1, d), mod, mod)


def _mm_kernel(x_ref, w_ref, o_ref, *, relu2):
    acc = jnp.dot(x_ref[...], w_ref[0].astype(_BF16), preferred_element_type=_F32)
    if relu2:
        acc = jnp.square(jnp.maximum(acc, 0.0))
    o_ref[...] = acc.astype(o_ref.dtype)


def _matmul(x, w_all, layer, *, tm=2048, tn=512, out_dtype, relu2=False, name):
    m, k = x.shape
    n = w_all.shape[2]
    return pl.pallas_call(
        functools.partial(_mm_kernel, relu2=relu2),
        grid=(m // tm, pl.cdiv(n, tn)),
        in_specs=[pl.BlockSpec((tm, k), lambda i, j: (i, 0), pipeline_mode=pl.Buffered(1)),
                  pl.BlockSpec((1, k, tn), lambda i, j: (layer, 0, j))],
        out_specs=pl.BlockSpec((tm, tn), lambda i, j: (i, j)),
        out_shape=jax.ShapeDtypeStruct((m, n), out_dtype),
        compiler_params=_params("parallel", "parallel"),
        name=name,
    )(x, w_all)


def _mm_nt_kernel(x_ref, wt_ref, o_ref):
    o_ref[...] = lax.dot_general(x_ref[...], wt_ref[0].astype(_BF16), _NT,
                                 preferred_element_type=_F32)


def _matmul_nt(x, wt_all, layer, n_out, *, tm=2048, tn=512, name):
    m, k = x.shape
    tn = min(tn, n_out)
    x_mode = dict(pipeline_mode=pl.Buffered(1)) if n_out > tn else {}
    return pl.pallas_call(
        _mm_nt_kernel,
        grid=(m // tm, n_out // tn),
        in_specs=[pl.BlockSpec((tm, k), lambda i, j: (i, 0), **x_mode),
                  pl.BlockSpec((1, tn, k), lambda i, j: (layer, j, 0))],
        out_specs=pl.BlockSpec((tm, tn), lambda i, j: (i, j)),
        out_shape=jax.ShapeDtypeStruct((m, n_out), _F32),
        compiler_params=_params("parallel", "parallel"),
        name=name,
    )(x, wt_all)


def _outproj_kernel(ym_ref, yg_ref, ys_ref, w_ref, x_ref, g_ref, o_ref):
    km = ym_ref.shape[-1]
    kg = yg_ref.shape[-1]
    w = w_ref[0].astype(_BF16)
    acc = jnp.dot(ym_ref[0], w[0:km, :], preferred_element_type=_F32)
    acc += jnp.dot(yg_ref[0], w[km:km + kg, :], preferred_element_type=_F32)
    acc += jnp.dot(ys_ref[0], w[km + kg:, :], preferred_element_type=_F32)
    o_ref[0] = x_ref[0] + g_ref[0] * acc


def _outproj(y_moba, y_gla, y_swa, w_all, layer, x, mod, gate_idx, *, tm=2048, tn=512):
    b, s, d = x.shape
    k = w_all.shape[1]
    km, kg, ks = y_moba.shape[-1], y_gla.shape[-1], y_swa.shape[-1]
    once = dict(pipeline_mode=pl.Buffered(1))
    return pl.pallas_call(
        _outproj_kernel,
        grid=(b, s // tm, d // tn),
        in_specs=[pl.BlockSpec((1, tm, km), lambda i, t, j: (i, t, 0), **once),
                  pl.BlockSpec((1, tm, kg), lambda i, t, j: (i, t, 0), **once),
                  pl.BlockSpec((1, tm, ks), lambda i, t, j: (i, t, 0), **once),
                  pl.BlockSpec((1, k, tn), lambda i, t, j: (layer, 0, j)),
                  pl.BlockSpec((1, tm, tn), lambda i, t, j: (i, t, j)),
                  pl.BlockSpec((1, 1, tn), lambda i, t, j: (i * 6 + gate_idx, 0, j))],
        out_specs=pl.BlockSpec((1, tm, tn), lambda i, t, j: (i, t, j)),
        out_shape=jax.ShapeDtypeStruct((b, s, d), _F32),
        compiler_params=_params("parallel", "parallel", "parallel"),
        name="out_proj",
    )(y_moba, y_gla, y_swa, w_all, x, mod)


def _mlp_out_kernel(h_ref, w_ref, x_ref, g_ref, o_ref, acc_ref, *, nk):
    kk = pl.program_id(3)
    j = pl.program_id(4)
    part = jnp.dot(h_ref[0], w_ref[0].astype(_BF16), preferred_element_type=_F32)

    @pl.when(kk == 0)
    def _():
        acc_ref[j] = part

    @pl.when((kk > 0) & (kk < nk - 1))
    def _():
        acc_ref[j] += part

    @pl.when(kk == nk - 1)
    def _():
        o_ref[0] = x_ref[0] + g_ref[0] * (acc_ref[j] + part)


def _mlp_out(h, w_all, layer, x, mod, gate_idx, *, tm=1024, tn=512, tk=4096, n_groups=2):
    b, s, d = x.shape
    f = w_all.shape[1]
    nk = f // tk
    nj = d // (n_groups * tn)
    assert nk >= 2

    def out_idx(i, t, gidx, k, j):
        return (i, t, gidx * nj + jnp.where(k == nk - 1, j, 0))

    return pl.pallas_call(
        functools.partial(_mlp_out_kernel, nk=nk),
        grid=(b, s // tm, n_groups, nk, nj),
        in_specs=[pl.BlockSpec((1, tm, tk), lambda i, t, gidx, k, j: (i, t, k)),
                  pl.BlockSpec((1, tk, tn), lambda i, t, gidx, k, j: (layer, k, gidx * nj + j)),
                  pl.BlockSpec((1, tm, tn), out_idx),
                  pl.BlockSpec((1, 1, tn),
                               lambda i, t, gidx, k, j: (i * 6 + gate_idx, 0, out_idx(i, t, gidx, k, j)[2]))],
        out_specs=pl.BlockSpec((1, tm, tn), out_idx),
        out_shape=jax.ShapeDtypeStruct((b, s, d), _F32),
        scratch_shapes=[pltpu.VMEM((nj, tm, tn), _F32)],
        compiler_params=_params("parallel", "parallel", "parallel", "arbitrary", "arbitrary"),
        name="mlp_out",
    )(h, w_all, x, mod)


def _moba_kernel(q_ref, k_ref, v_ref, qg_ref, kg_ref, slope_ref, o_ref,
                 vaug, bias_s, l_s, p_s, *, blk, topk):
    s_len, d = q_ref.shape[1], q_ref.shape[2]
    nb = s_len // blk
    q = _rms(q_ref[0], qg_ref[...])
    k = _rms(k_ref[0], kg_ref[...])
    kmean = jnp.sum(k.reshape(nb, blk, d), axis=1) * (1.0 / blk)
    gate_t = lax.dot_general(kmean, q, _NT, precision=lax.Precision.HIGHEST,
                             preferred_element_type=_F32)
    qb = q.astype(_BF16)
    kb = k.astype(_BF16)
    vaug[:, :d] = v_ref[0].astype(_BF16)
    vaug[:, d:] = jnp.ones((s_len, d), _BF16)

    slope2 = slope_ref[0][:, 0:1] * _LOG2E
    c1 = (d ** -0.5) * _LOG2E
    key_i = lax.broadcasted_iota(jnp.int32, (blk, blk), 0)
    qry_i = lax.broadcasted_iota(jnp.int32, (blk, blk), 1)
    rel = (qry_i - key_i).astype(_F32)
    bias_s[0] = jnp.where(qry_i >= key_i, slope2 * rel, jnp.inf)
    for dd in range(1, nb):
        bias_s[dd] = slope2 * (rel + float(dd * blk))
    cand = lax.broadcasted_iota(jnp.int32, (nb, blk), 0)

    for i in range(nb):
        n_keys = (i + 1) * blk
        qi = qb[i * blk:(i + 1) * blk]
        s_t = lax.dot_general(kb[:n_keys], qi, _NT, preferred_element_type=_F32)
        sel_bias = None
        if i > topk:
            g = gate_t[:, i * blk:(i + 1) * blk]
            rank = jnp.zeros((nb, blk), jnp.int32)
            for jp in range(i):
                gj = g[jp:jp + 1, :]
                beats = (gj > g) | ((gj == g) & (cand > jp))
                rank += beats.astype(jnp.int32)
            sel_bias = jnp.where(rank < topk, 0.0, -jnp.inf)
        m8 = jnp.full((8, blk), -jnp.inf, _F32)
        for j in range(i + 1):
            t = s_t[j * blk:(j + 1) * blk] * c1 - bias_s[i - j]
            if sel_bias is not None and j < i:
                t = t + sel_bias[j:j + 1, :]
            l_s[j * blk:(j + 1) * blk, :] = t
            m8 = jnp.maximum(m8, jnp.max(t.reshape(blk // 8, 8, blk), axis=0))
        m = jnp.max(m8, axis=0, keepdims=True)
        for j in range(i + 1):
            p_s[j * blk:(j + 1) * blk, :] = jnp.exp2(l_s[j * blk:(j + 1) * blk, :] - m).astype(_BF16)
        o2 = lax.dot_general(p_s[0:n_keys, :], vaug[0:n_keys, :], _TN,
                             preferred_element_type=_F32)
        o_ref[0, i * blk:(i + 1) * blk, :] = (o2[:, :d] / o2[:, d:]).astype(o_ref.dtype)


def _moba(proj, q_gain, k_gain, slopes, *, heads, q_col, k_col, v_col):
    b, s, _ = proj.shape
    d = HEAD_DIM
    blk = MOBA_BLOCK
    blk_spec = lambda c0: pl.BlockSpec((1, s, d), lambda i, h: (i, 0, c0 // d + h))
    return pl.pallas_call(
        functools.partial(_moba_kernel, blk=blk, topk=MOBA_TOPK),
        grid=(b, heads),
        in_specs=[blk_spec(q_col), blk_spec(k_col), blk_spec(v_col),
                  pl.BlockSpec((1, d), lambda i, h: (0, 0)),
                  pl.BlockSpec((1, d), lambda i, h: (0, 0)),
                  pl.BlockSpec((1, 1, LANES), lambda i, h: (h, 0, 0))],
        out_specs=pl.BlockSpec((1, s, d), lambda i, h: (i, 0, h)),
        out_shape=jax.ShapeDtypeStruct((b, s, heads * d), _BF16),
        scratch_shapes=[pltpu.VMEM((s, 2 * d), _BF16),
                        pltpu.VMEM((s // blk, blk, blk), _F32),
                        pltpu.VMEM((s, blk), _F32),
                        pltpu.VMEM((s, blk), _BF16)],
        compiler_params=_params("parallel", "parallel"),
        name="moba_attn",
    )(proj, proj, proj, q_gain.reshape(1, d), k_gain.reshape(1, d), slopes)


def _swa_kernel(q_ref, k_ref, v_ref, qg_ref, kg_ref, slope_ref, sink_ref, o_ref,
                qbuf, kbuf, vbuf, bias_s, *, win, groups):
    s_len, d = k_ref.shape[1], k_ref.shape[2]
    nq_rows = groups * win
    kbuf[0:win] = jnp.zeros((win, d), _BF16)
    kbuf[win:] = _rms(k_ref[0], kg_ref[...]).astype(_BF16)
    vbuf[0:win] = jnp.zeros((win, d), _BF16)
    vbuf[win:] = v_ref[0].astype(_BF16)
    for g in range(groups):
        qbuf[g] = _rms(q_ref[0, :, g * d:(g + 1) * d], qg_ref[...]).astype(_BF16)

    key_i = lax.broadcasted_iota(jnp.int32, (2 * win, nq_rows), 0)
    qry_i = lax.broadcasted_iota(jnp.int32, (2 * win, nq_rows), 1) % win
    dist_i = win + qry_i - key_i
    band = (dist_i >= 0) & (dist_i < win)
    slope_row = jnp.concatenate(
        [jnp.broadcast_to(slope_ref[0][g:g + 1, :], (1, win)) for g in range(groups)], axis=1)
    sink2 = jnp.concatenate(
        [jnp.broadcast_to(sink_ref[0][g:g + 1, :], (1, win)) for g in range(groups)], axis=1) * _LOG2E
    alibi2 = (slope_row * _LOG2E) * dist_i.astype(_F32)
    bias_s[0] = jnp.where(band & (key_i >= win), alibi2, jnp.inf)
    bias_s[1] = jnp.where(band, alibi2, jnp.inf)
    c1 = (d ** -0.5) * _LOG2E

    def body(n, carry):
        st = pl.multiple_of(n * win, win)
        qs = jnp.concatenate([qbuf[g, pl.ds(st, win), :] for g in range(groups)], axis=0)
        kw = kbuf[pl.ds(st, 2 * win), :]
        vw = vbuf[pl.ds(st, 2 * win), :]
        s_t = lax.dot_general(kw, qs, _NT, preferred_element_type=_F32)
        l2 = s_t * c1 - bias_s[jnp.minimum(n, 1)]
        m = jnp.maximum(jnp.max(l2, axis=0, keepdims=True), sink2)
        p = jnp.exp2(l2 - m)
        den = jnp.sum(p, axis=0, keepdims=True) + jnp.exp2(sink2 - m)
        pn = (p * (1.0 / den)).astype(_BF16)
        o = lax.dot_general(pn, vw, _TN, preferred_element_type=_F32)
        for g in range(groups):
            o_ref[0, pl.ds(st, win), g * d:(g + 1) * d] = o[g * win:(g + 1) * win].astype(o_ref.dtype)
        return carry

    lax.fori_loop(0, s_len // win, body, 0, unroll=4)


def _swa(proj, q_gain, k_gain, slopes, sinks, *, kv_heads, groups, q_col, k_col, v_col):
    b, s, _ = proj.shape
    d = HEAD_DIM
    win = SWA_WINDOW
    return pl.pallas_call(
        functools.partial(_swa_kernel, win=win, groups=groups),
        grid=(b, kv_heads),
        in_specs=[pl.BlockSpec((1, s, groups * d), lambda i, h: (i, 0, q_col // (groups * d) + h)),
                  pl.BlockSpec((1, s, d), lambda i, h: (i, 0, k_col // d + h)),
                  pl.BlockSpec((1, s, d), lambda i, h: (i, 0, v_col // d + h)),
                  pl.BlockSpec((1, d), lambda i, h: (0, 0)),
                  pl.BlockSpec((1, d), lambda i, h: (0, 0)),
                  pl.BlockSpec((1, groups, LANES), lambda i, h: (h, 0, 0)),
                  pl.BlockSpec((1, groups, LANES), lambda i, h: (h, 0, 0))],
        out_specs=pl.BlockSpec((1, s, groups * d), lambda i, h: (i, 0, h)),
        out_shape=jax.ShapeDtypeStruct((b, s, kv_heads * groups * d), _BF16),
        scratch_shapes=[pltpu.VMEM((groups, s, d), _BF16),
                        pltpu.VMEM((s + win, d), _BF16),
                        pltpu.VMEM((s + win, d), _BF16),
                        pltpu.VMEM((2, 2 * win, groups * win), _F32)],
        compiler_params=_params("parallel", "parallel"),
        name="swa_attn",
    )(proj, proj, proj, q_gain.reshape(1, d), k_gain.reshape(1, d), slopes, sinks)


def _gla_kernel(q_ref, k_ref, v_ref, g_ref, a_ref, wa_ref, ba_ref, gn_ref, o_ref,
                qe_s, ke_s, kd_s, v_s, dec_s, o_s, st_s, *, chunk, dk, dv, grp):
    s_len = q_ref.shape[1]
    n_chunks = s_len // chunk
    pair_k = q_ref.shape[2]
    pair_v = v_ref.shape[2]

    z = jnp.dot(a_ref[0], wa_ref[...], precision=lax.Precision.HIGHEST,
                preferred_element_type=_F32) + ba_ref[...]
    log_a = (jnp.minimum(z, 0.0) - jnp.log1p(jnp.exp(-jnp.abs(z)))) * (1.0 / GLA_TAU)
    rin = lax.broadcasted_iota(jnp.int32, (s_len, pair_k), 0) % chunk
    lam = log_a
    shift = 1
    while shift < chunk:
        lam = lam + jnp.where(rin >= shift, pltpu.roll(lam, shift, 0), 0.0)
        shift *= 2
    last3 = lam.reshape(n_chunks, chunk, pair_k)[:, chunk - 1:chunk, :]
    dec3 = jnp.exp(last3)
    dec_rows = jnp.broadcast_to(dec3, (n_chunks, chunk, pair_k)).reshape(s_len, pair_k)
    ke = k_ref[0] * jnp.exp(-lam)
    qe_s[...] = (q_ref[0] * (dk ** -0.5) * jnp.exp(lam)).astype(_BF16)
    ke_s[...] = ke.astype(_BF16)
    kd_s[...] = (ke * dec_rows).astype(_BF16)
    dec_s[...] = dec3
    v_s[...] = v_ref[0].astype(_BF16)

    rows = grp * chunk
    row_c = lax.broadcasted_iota(jnp.int32, (rows, rows), 0)
    col_c = lax.broadcasted_iota(jnp.int32, (rows, rows), 1)
    intra_mask = (row_c // chunk == col_c // chunk) & (row_c >= col_c)
    head0 = lax.broadcasted_iota(jnp.int32, (rows, pair_k), 1) < dk
    row_chunk = lax.broadcasted_iota(jnp.int32, (rows, pair_k), 0) // chunk
    srow = lax.broadcasted_iota(jnp.int32, (pair_v, grp * pair_k), 0)
    scol = lax.broadcasted_iota(jnp.int32, (pair_v, grp * pair_k), 1) % pair_k
    same_head = (srow < dv) == (scol < dk)
    zero_b = jnp.zeros((), _BF16)

    def widen(t):
        return jnp.concatenate([jnp.where(row_chunk == c, t, zero_b) for c in range(grp)], axis=1)

    def intra(gi, carry):
        r0 = pl.multiple_of(gi * rows, rows)
        qe = qe_s[pl.ds(r0, rows), :]
        ke_g = ke_s[pl.ds(r0, rows), :]
        vg = v_s[pl.ds(r0, rows), :]
        att0 = lax.dot_general(jnp.where(head0, qe, zero_b), ke_g, _NT, preferred_element_type=_F32)
        att1 = lax.dot_general(jnp.where(head0, zero_b, qe), ke_g, _NT, preferred_element_type=_F32)
        o0 = jnp.dot(jnp.where(intra_mask, att0, 0.0).astype(_BF16), vg[:, :dv],
                     preferred_element_type=_F32)
        o1 = jnp.dot(jnp.where(intra_mask, att1, 0.0).astype(_BF16), vg[:, dv:],
                     preferred_element_type=_F32)
        o_s[pl.ds(r0, rows), :] = jnp.concatenate([o0, o1], axis=1)
        upd = lax.dot_general(vg, widen(kd_s[pl.ds(r0, rows), :]), _TN,
                              preferred_element_type=_F32)
        st_s[gi] = jnp.where(same_head, upd, 0.0)
        return carry

    lax.fori_loop(0, n_chunks // grp, intra, 0, unroll=2)

    def scan(gi, state):
        for c in range(grp):
            upd = st_s[gi, :, c * pair_k:(c + 1) * pair_k]
            st_s[gi, :, c * pair_k:(c + 1) * pair_k] = state
            state = state * dec_s[gi * grp + c] + upd
        return state

    lax.fori_loop(0, n_chunks // grp, scan, jnp.zeros((pair_v, pair_k), _F32))

    def inter(gi, carry):
        for c in range(grp):
            r0 = pl.multiple_of(gi * rows + c * chunk, chunk)
            state = st_s[gi, :, c * pair_k:(c + 1) * pair_k].astype(_BF16)
            o_s[pl.ds(r0, chunk), :] += lax.dot_general(
                qe_s[pl.ds(r0, chunk), :], state, _NT, preferred_element_type=_F32)
        return carry

    lax.fori_loop(0, n_chunks // grp, inter, 0, unroll=2)

    gate = g_ref[0]
    for h in range(pair_v // dv):
        o = _rms(o_s[:, h * dv:(h + 1) * dv], gn_ref[...])
        gh = gate[:, h * dv:(h + 1) * dv]
        o_ref[0, :, h * dv:(h + 1) * dv] = (o * (gh * jax.nn.sigmoid(gh))).astype(o_ref.dtype)


def _gla(proj, low, w_a_pad, b_a, out_gain, *, heads, q_col, k_col, v_col, g_col):
    b, s, _ = proj.shape
    pairs = heads // 2
    pk, pv = 2 * GLA_DK, 2 * GLA_DV
    n_chunks = s // GLA_CHUNK
    return pl.pallas_call(
        functools.partial(_gla_kernel, chunk=GLA_CHUNK, dk=GLA_DK, dv=GLA_DV, grp=GLA_GROUP),
        grid=(b, pairs),
        in_specs=[pl.BlockSpec((1, s, pk), lambda i, p: (i, 0, q_col // pk + p)),
                  pl.BlockSpec((1, s, pk), lambda i, p: (i, 0, k_col // pk + p)),
                  pl.BlockSpec((1, s, pv), lambda i, p: (i, 0, v_col // pv + p)),
                  pl.BlockSpec((1, s, pv), lambda i, p: (i, 0, g_col // pv + p)),
                  pl.BlockSpec((1, s, LANES), lambda i, p: (i, 0, 0)),
                  pl.BlockSpec((LANES, pk), lambda i, p: (0, p)),
                  pl.BlockSpec((1, pk), lambda i, p: (0, p)),
                  pl.BlockSpec((1, GLA_DV), lambda i, p: (0, 0))],
        out_specs=pl.BlockSpec((1, s, pv), lambda i, p: (i, 0, p)),
        out_shape=jax.ShapeDtypeStruct((b, s, heads * GLA_DV), _BF16),
        scratch_shapes=[pltpu.VMEM((s, pk), _BF16), pltpu.VMEM((s, pk), _BF16),
                        pltpu.VMEM((s, pk), _BF16), pltpu.VMEM((s, pv), _BF16),
                        pltpu.VMEM((n_chunks, 1, pk), _F32), pltpu.VMEM((s, pv), _F32),
                        pltpu.VMEM((n_chunks // GLA_GROUP, pv, GLA_GROUP * pk), _F32)],
        compiler_params=_params("parallel", "parallel"),
        name="gla_mixer",
    )(proj, proj, proj, proj, low, w_a_pad, b_a.reshape(1, -1), out_gain.reshape(1, -1))


def _alibi_slopes(n_heads):
    return jnp.exp2(-ALIBI_MAX_EXP * jnp.arange(1, n_heads + 1, dtype=_F32) / n_heads)


def _lane_rows(vec, lead):
    return jnp.broadcast_to(vec.astype(_F32)[:, None], (vec.shape[0], LANES)).reshape(lead, -1, LANES)


def kernel(x, c, w_ada, b_ada, norm_attn, w_in, moba_q_norm, moba_k_norm, gla_w_a, gla_b_a,
           gla_out_norm, swa_q_norm, swa_k_norm, swa_sinks, w_out, norm_mlp, w_mlp_in, w_mlp_out):
    b, s, d = x.shape
    depth = w_ada.shape[0]
    d_ff = w_mlp_in.shape[-1]
    swa_heads = swa_sinks.shape[-1]
    gla_heads = gla_w_a.shape[-1] // GLA_DK
    moba_heads = (d - gla_heads * GLA_DV - swa_heads * HEAD_DIM) // HEAD_DIM
    groups = swa_heads // SWA_KV_HEADS

    n_moba = moba_heads * HEAD_DIM
    mq_c, mk_c, mv_c = 0, n_moba, 2 * n_moba
    gq_c = 3 * n_moba
    gk_c = gq_c + gla_heads * GLA_DK
    gv_c = gk_c + gla_heads * GLA_DK
    gg_c = gv_c + gla_heads * GLA_DV
    ga_c = gg_c + gla_heads * GLA_DV
    sq_c = ga_c + GLA_LOWRANK
    n_swa = (swa_heads + 2 * SWA_KV_HEADS) * HEAD_DIM
    sk_c = swa_heads * HEAD_DIM
    sv_c = sk_c + SWA_KV_HEADS * HEAD_DIM

    w_in_t = jnp.swapaxes(w_in, 1, 2)

    moba_slopes = _lane_rows(_alibi_slopes(moba_heads), moba_heads)
    swa_slopes = _lane_rows(_alibi_slopes(swa_heads), SWA_KV_HEADS)

    c_pad = jnp.pad(c, ((0, 8 - b), (0, 0)))
    mod_all = _ada(c_pad, w_ada, b_ada)

    for l in range(depth):
        mod = mod_all[l, :b].reshape(b * 6, 1, d)
        w_a_pad = jnp.pad(gla_w_a[l], ((0, LANES - GLA_LOWRANK), (0, 0)))
        w_swa_t = w_in_t[l, sq_c:sq_c + n_swa][None]
        w_low_t = jnp.pad(w_in_t[l, ga_c:sq_c], ((0, LANES - GLA_LOWRANK), (0, 0)))[None]

        h = _norm_mod(x, norm_attn[l], mod, 1, 0).reshape(b * s, d)
        proj = _matmul_nt(h, w_in_t, l, ga_c, name="in_proj").reshape(b, s, -1)
        proj_swa = _matmul_nt(h, w_swa_t, 0, n_swa, name="in_proj_swa").reshape(b, s, -1)
        low = _matmul_nt(h, w_low_t, 0, LANES, tm=1024, name="in_proj_low").reshape(b, s, -1)

        y_moba = _moba(proj, moba_q_norm[l], moba_k_norm[l], moba_slopes,
                       heads=moba_heads, q_col=mq_c, k_col=mk_c, v_col=mv_c)
        y_gla = _gla(proj, low, w_a_pad, gla_b_a[l], gla_out_norm[l], heads=gla_heads,
                     q_col=gq_c, k_col=gk_c, v_col=gv_c, g_col=gg_c)
        y_swa = _swa(proj_swa, swa_q_norm[l], swa_k_norm[l], swa_slopes,
                     _lane_rows(swa_sinks[l], SWA_KV_HEADS),
                     kv_heads=SWA_KV_HEADS, groups=groups, q_col=0, k_col=sk_c, v_col=sv_c)
        x = _outproj(y_moba, y_gla, y_swa, w_out, l, x, mod, 2)

        h = _norm_mod(x, norm_mlp[l], mod, 4, 3).reshape(b * s, d)
        hid = _matmul(h, w_mlp_in, l, out_dtype=_BF16, relu2=True, name="mlp_in")
        x = _mlp_out(hid.reshape(b, s, d_ff), w_mlp_out, l, x, mod, 5)
    return x
```
